```python
import math
import jax, jax.numpy as jnp
from jax import lax
import numpy as np

D_MODEL = 2048
BATCH = 4
SEQ = 4096
DEPTH = 1

GDN_HEADS = 8
GDN_DK = 128
GDN_DV = 128
CONV_WIDTH = 4
CHUNK = 64
DSA_HEADS = 8
DSA_KV_HEADS = 2
DSA_HEAD_DIM = 128
IDX_HEADS = 16
IDX_DIM = 64
TOPK_MAX = 256
Q_BLOCK = 128
ROPE_THETA = 500000.0
ROPE_FRACTION = 4
EPS = 1e-6

GDN_W = GDN_HEADS * GDN_DV
DSA_W = DSA_HEADS * DSA_HEAD_DIM
MIX_W = GDN_W + DSA_W
CONV_CH = 2 * GDN_HEADS * GDN_DK + GDN_HEADS * GDN_DV
IN_SIZES = (
    GDN_HEADS * GDN_DK,
    GDN_HEADS * GDN_DK,
    GDN_HEADS * GDN_DV,
    GDN_W,
    GDN_HEADS,
    GDN_HEADS,
    DSA_HEADS * DSA_HEAD_DIM,
    DSA_KV_HEADS * DSA_HEAD_DIM,
    DSA_KV_HEADS * DSA_HEAD_DIM,
    DSA_W,
    IDX_HEADS * IDX_DIM,
    IDX_DIM,
    IDX_HEADS,
)
IN_COLS = sum(IN_SIZES)

kernel_name = "hymba_gdn_dsa_hybrid_layer"


def rms_norm(x, g):
    xf = x.astype(jnp.float32)
    out = xf * lax.rsqrt(jnp.mean(xf * xf, axis=-1, keepdims=True) + EPS)
    return (out * g.astype(jnp.float32)).astype(x.dtype)


def l2_normalize(x):
    return x * lax.rsqrt(jnp.sum(x * x, axis=-1, keepdims=True) + EPS)


def partial_rope(x, positions):
    d = x.shape[-1]
    rot = d // ROPE_FRACTION
    half = rot // 2
    inv_freq = ROPE_THETA ** (-(jnp.arange(half, dtype=jnp.float32) * 2.0 / rot))
    ang = positions.astype(jnp.float32)[..., None] * inv_freq
    cos = jnp.cos(ang)[:, :, None, :]
    sin = jnp.sin(ang)[:, :, None, :]
    x1, x2, rest = x[..., :half], x[..., half:rot], x[..., rot:]
    return jnp.concatenate([x1 * cos - x2 * sin, x2 * cos + x1 * sin, rest], axis=-1)


def causal_depthwise_conv(x, w):
    c = x.shape[-1]
    return lax.conv_general_dilated(
        x, w[:, None, :].astype(x.dtype), window_strides=(1,), padding=[(w.shape[0] - 1, 0)],
        dimension_numbers=('NWC', 'WIO', 'NWC'), feature_group_count=c)


def gated_delta_rule(q, k, v, g, beta):
    b, t, h, dk = q.shape
    dv = v.shape[-1]
    n = t // CHUNK
    q = l2_normalize(q) * (dk ** -0.5)
    k = l2_normalize(k)

    def to_chunks(a):
        return a.reshape(b, n, CHUNK, h, a.shape[-1]).transpose(1, 0, 3, 2, 4)

    qc, kc, vc = to_chunks(q), to_chunks(k), to_chunks(v)
    gc = jnp.cumsum(g.reshape(b, n, CHUNK, h).transpose(1, 0, 3, 2), axis=-1)
    bc = beta.reshape(b, n, CHUNK, h).transpose(1, 0, 3, 2)
    pos = jnp.arange(CHUNK)
    incl = pos[:, None] >= pos[None, :]
    strict = pos[:, None] > pos[None, :]
    diff = gc[..., :, None] - gc[..., None, :]
    decay = jnp.where(incl, jnp.exp(jnp.where(incl, diff, 0.0)), 0.0)
    kb = kc * bc[..., None]
    vb = vc * bc[..., None]
    lmat = jnp.where(strict, jnp.einsum('nbhid,nbhjd->nbhij', kb, kc) * decay, 0.0)
    eye = jnp.eye(CHUNK, dtype=q.dtype)
    tmat = lax.linalg.triangular_solve(lmat + eye, jnp.broadcast_to(eye, lmat.shape),
                                       left_side=True, lower=True, unit_diagonal=True)
    w = jnp.einsum('nbhij,nbhjd->nbhid', tmat, kb * jnp.exp(gc)[..., None])
    u = jnp.einsum('nbhij,nbhjd->nbhid', tmat, vb)
    a_intra = jnp.einsum('nbhid,nbhjd->nbhij', qc, kc) * decay

    def step(state, xs):
        qn, kn, un, wn, gn, an = xs
        v_new = un - jnp.einsum('bhcd,bhde->bhce', wn, state)
        o = (jnp.einsum('bhcd,bhde->bhce', qn * jnp.exp(gn)[..., None], state)
             + jnp.einsum('bhij,bhje->bhie', an, v_new))
        g_last = gn[..., -1]
        state = (state * jnp.exp(g_last)[..., None, None]
                 + jnp.einsum('bhcd,bhce->bhde', kn * jnp.exp(g_last[..., None] - gn)[..., None], v_new))
        return state, o

    s0 = jnp.zeros((b, h, dk, dv), q.dtype)
    _, o = lax.scan(step, s0, (qc, kc, u, w, gc, a_intra))
    return o.transpose(1, 0, 3, 2, 4).reshape(b, t, h, dv)


def dsa_sparse_attention(q, k, v, q_idx, k_idx, w_idx, k_sel):
    b, t, hq, dh = q.shape
    hkv = k.shape[2]
    rep = hq // hkv
    nb = t // Q_BLOCK
    qg = q.reshape(b, t, hkv, rep, dh)
    key_pos = jnp.arange(t)
    scale = dh ** -0.5
    gather = jax.vmap(lambda src, ids: src[ids])

    def block(j):
        t0 = j * Q_BLOCK
        qi = lax.dynamic_slice_in_dim(q_idx, t0, Q_BLOCK, axis=1)
        wi = lax.dynamic_slice_in_dim(w_idx, t0, Q_BLOCK, axis=1)
        qa = lax.dynamic_slice_in_dim(qg, t0, Q_BLOCK, axis=1)
        q_pos = t0 + jnp.arange(Q_BLOCK)
        score = jnp.einsum('bqh,bqhs->bqs', wi,
                           jax.nn.relu(jnp.einsum('bqhd,bsd->bqhs', qi, k_idx)))
        causal = key_pos[None, :] <= q_pos[:, None]
        score = jnp.where(causal[None], score, -jnp.inf)
        _, sel = lax.top_k(score, k_sel)
        valid = sel <= q_pos[None, :, None]
        k_g = gather(k, sel)
        v_g = gather(v, sel)
        logits = jnp.einsum('bqgrd,bqkgd->bqgrk', qa, k_g) * scale
        logits = jnp.where(valid[:, :, None, None, :], logits, -jnp.inf)
        p = jax.nn.softmax(logits, axis=-1)
        o = jnp.einsum('bqgrk,bqkgd->bqgrd', p, v_g)
        return o.reshape(b, Q_BLOCK, hq * dh)

    out = lax.map(block, jnp.arange(nb))
    return out.transpose(1, 0, 2, 3).reshape(b, t, hq * dh)


def setup_inputs(seed: int = 0) -> dict:
    key = jax.random.key(seed)
    ks = jax.random.split(key, 10)
    x = jax.random.normal(ks[0], (BATCH, SEQ, D_MODEL), jnp.float32)
    positions = jnp.broadcast_to(jnp.arange(SEQ, dtype=jnp.int32), (BATCH, SEQ))
    attn_norm_g = 1.0 + 0.02 * jax.random.normal(ks[1], (DEPTH, D_MODEL), jnp.float32)
    w_in = jax.random.normal(ks[2], (DEPTH, D_MODEL, IN_COLS), jnp.float32) * D_MODEL ** -0.5
    gdn_conv_w = jax.random.normal(ks[3], (DEPTH, CONV_WIDTH, CONV_CH), jnp.float32) * CONV_WIDTH ** -0.5
    gdn_a_log = jnp.log(jax.random.uniform(ks[4], (DEPTH, GDN_HEADS), jnp.float32, 1.0, 16.0))
    dt = jnp.exp(jax.random.uniform(ks[5], (DEPTH, GDN_HEADS), jnp.float32,
                                    math.log(1e-3), math.log(1e-1)))
    gdn_dt_bias = dt + jnp.log(-jnp.expm1(-dt))
    gdn_norm_g = 1.0 + 0.02 * jax.random.normal(ks[6], (DEPTH, GDN_DV), jnp.float32)
    w_out = jax.random.normal(ks[7], (DEPTH, MIX_W, D_MODEL), jnp.float32) * MIX_W ** -0.5
    final_norm_g = 1.0 + 0.02 * jax.random.normal(ks[8], (D_MODEL,), jnp.float32)
    return {"x": x, "positions": positions, "attn_norm_g": attn_norm_g, "w_in": w_in,
            "gdn_conv_w": gdn_conv_w, "gdn_a_log": gdn_a_log, "gdn_dt_bias": gdn_dt_bias,
            "gdn_norm_g": gdn_norm_g, "w_out": w_out, "final_norm_g": final_norm_g}


def reference(x, positions, attn_norm_g, w_in, gdn_conv_w, gdn_a_log, gdn_dt_bias,
              gdn_norm_g, w_out, final_norm_g):
    b, t, _ = x.shape
    k_sel = min(TOPK_MAX, t // 4)
    split_points = tuple(int(s) for s in np.cumsum(IN_SIZES)[:-1])
    f32 = jnp.float32
    for l in range(DEPTH):
        h = rms_norm(x, attn_norm_g[l])
        proj = (h @ w_in[l]).astype(f32)
        (gq, gk, gv, gz, ga, gb, aq, ak, av, az, iq, ik, iw) = jnp.split(proj, split_points, axis=-1)

        qkv = jax.nn.silu(causal_depthwise_conv(jnp.concatenate([gq, gk, gv], axis=-1), gdn_conv_w[l].astype(f32)))
        q_a, k_a, v_a = jnp.split(qkv, (GDN_HEADS * GDN_DK, 2 * GDN_HEADS * GDN_DK), axis=-1)
        q_a = q_a.reshape(b, t, GDN_HEADS, GDN_DK)
        k_a = k_a.reshape(b, t, GDN_HEADS, GDN_DK)
        v_a = v_a.reshape(b, t, GDN_HEADS, GDN_DV)
        g_a = -jnp.exp(gdn_a_log[l].astype(f32)) * jax.nn.softplus(ga + gdn_dt_bias[l].astype(f32))
        beta_a = jax.nn.sigmoid(gb)
        o_a = gated_delta_rule(q_a, k_a, v_a, g_a, beta_a)
        o_a = rms_norm(o_a, gdn_norm_g[l]).reshape(b, t, GDN_W) * jax.nn.silu(gz)

        q_b = partial_rope(aq.reshape(b, t, DSA_HEADS, DSA_HEAD_DIM), positions)
        k_b = partial_rope(ak.reshape(b, t, DSA_KV_HEADS, DSA_HEAD_DIM), positions)
        v_b = av.reshape(b, t, DSA_KV_HEADS, DSA_HEAD_DIM)
        q_i = partial_rope(iq.reshape(b, t, IDX_HEADS, IDX_DIM), positions)
        k_i = partial_rope(ik.reshape(b, t, 1, IDX_DIM), positions)[:, :, 0]
        w_i = iw * (IDX_HEADS ** -0.5) * (IDX_DIM ** -0.5)
        o_b = dsa_sparse_attention(q_b, k_b, v_b, q_i, k_i, w_i, k_sel) * jax.nn.silu(az)

        mixed = jnp.concatenate([o_a, o_b], axis=-1).astype(x.dtype) @ w_out[l]
        x = x + mixed
    return rms_norm(x, final_norm_g)
```

```python
import functools
import math

import jax
import jax.numpy as jnp
from jax import lax
from jax.experimental import pallas as pl
from jax.experimental.pallas import tpu as pltpu

F32 = jnp.float32
BF16 = jnp.bfloat16
I32 = jnp.int32

D_MODEL = 2048
GDN_HEADS = 8
GDN_DK = 128
GDN_DV = 128
CONV_WIDTH = 4
CHUNK = 64
DSA_HEADS = 8
DSA_KV_HEADS = 2
DSA_HEAD_DIM = 128
IDX_HEADS = 16
IDX_DIM = 64
TOPK_MAX = 256
ROPE_THETA = 500000.0
ROPE_FRACTION = 4
EPS = 1e-6

LANES = 128
VMEM_LIMIT = 56 * 1024 * 1024

OFF_GQ, OFF_GK, OFF_GV, OFF_GZ = 0, 1024, 2048, 3072
OFF_AQ, OFF_AZ, OFF_IQ, OFF_AK, OFF_AV = 4096, 5120, 6144, 7168, 7424
MAIN_COLS = 7680
SM_IK, SM_GA, SM_GB, SM_IW = 0, 64, 72, 80

INT_MIN = -2147483648
KEY_F32_LOWEST = INT_MIN + 0x00800000
NEG_BIG = -1e30


def _sigmoid(x):
    return 1.0 / (1.0 + jnp.exp(-x))


def _softplus(x):
    return jnp.maximum(x, 0.0) + jnp.log(1.0 + jnp.exp(-jnp.abs(x)))


def _dot(a, b):
    return jnp.dot(a, b, preferred_element_type=F32)


def _dot_nt(a, b):
    return lax.dot_general(a, b, (((1,), (1,)), ((), ())), preferred_element_type=F32)


def _dot_tn(a, b):
    return lax.dot_general(a, b, (((0,), (0,)), ((), ())), preferred_element_type=F32)


PROJ_TM = 512
PROJ_TN = 512


def _rope_slab(xs, tab_ref, half):
    n = xs.shape[-1]
    return (xs * tab_ref[0] + pltpu.roll(xs, half, 1) * tab_ref[1]
            + pltpu.roll(xs, n - half, 1) * tab_ref[2])


def _proj_kernel(x_ref, g_ref, w_ref, ws_ref, tq_ref, tk_ref, ti_ref, main_ref, small_ref, h_ref):
    j = pl.program_id(1)

    @pl.when(j == 0)
    def _():
        x = x_ref[...]
        ms = jnp.mean(x * x, axis=-1, keepdims=True)
        h = (x * lax.rsqrt(ms + EPS) * g_ref[...]).astype(BF16)
        h_ref[...] = h
        s = _dot(h, ws_ref[...])
        roped = _rope_slab(s, ti_ref, IDX_DIM // ROPE_FRACTION // 2)
        lane = lax.broadcasted_iota(I32, s.shape, 1)
        small_ref[...] = jnp.where(lane < IDX_DIM, roped, s)

    acc = _dot(h_ref[...], w_ref[...])
    nslab = PROJ_TN // LANES
    jq0 = OFF_AQ // PROJ_TN
    ji0 = OFF_IQ // PROJ_TN
    jk = OFF_AK // PROJ_TN
    is_q = (j == jq0) | (j == jq0 + 1)
    is_i = (j == ji0) | (j == ji0 + 1)
    is_k = j == jk

    def slab(c):
        return acc[:, c * LANES:(c + 1) * LANES]

    @pl.when(is_q)
    def _():
        for c in range(nslab):
            main_ref[:, c * LANES:(c + 1) * LANES] = _rope_slab(
                slab(c), tq_ref, DSA_HEAD_DIM // ROPE_FRACTION // 2).astype(BF16)

    @pl.when(is_i)
    def _():
        for c in range(nslab):
            main_ref[:, c * LANES:(c + 1) * LANES] = _rope_slab(
                slab(c), ti_ref, IDX_DIM // ROPE_FRACTION // 2).astype(BF16)

    @pl.when(is_k)
    def _():
        for c in range(nslab):
            if c < DSA_KV_HEADS:
                val = _rope_slab(slab(c), tk_ref, DSA_HEAD_DIM // ROPE_FRACTION // 2)
            else:
                val = slab(c)
            main_ref[:, c * LANES:(c + 1) * LANES] = val.astype(BF16)

    @pl.when(jnp.logical_not(is_q | is_i | is_k))
    def _():
        main_ref[...] = acc.astype(BF16)


def _input_projection(xf, g, w_main, w_small, tab_q, tab_k, tab_i):
    n = xf.shape[0]
    grid = (n // PROJ_TM, MAIN_COLS // PROJ_TN)
    tab_spec = pl.BlockSpec((3, PROJ_TM, LANES), lambda i, j: (0, i, 0))
    return pl.pallas_call(
        _proj_kernel,
        grid=grid,
        in_specs=[
            pl.BlockSpec((PROJ_TM, D_MODEL), lambda i, j: (i, 0)),
            pl.BlockSpec((1, D_MODEL), lambda i, j: (0, 0)),
            pl.BlockSpec((D_MODEL, PROJ_TN), lambda i, j: (0, j)),
            pl.BlockSpec((D_MODEL, LANES), lambda i, j: (0, 0)),
            tab_spec, tab_spec, tab_spec,
        ],
        out_specs=[
            pl.BlockSpec((PROJ_TM, PROJ_TN), lambda i, j: (i, j)),
            pl.BlockSpec((PROJ_TM, LANES), lambda i, j: (i, 0)),
        ],
        out_shape=[
            jax.ShapeDtypeStruct((n, MAIN_COLS), BF16),
            jax.ShapeDtypeStruct((n, LANES), F32),
        ],
        scratch_shapes=[pltpu.VMEM((PROJ_TM, D_MODEL), BF16)],
        compiler_params=pltpu.CompilerParams(
            dimension_semantics=("arbitrary", "arbitrary"), vmem_limit_bytes=VMEM_LIMIT),
        name="in_proj",
    )(xf, g, w_main, w_small, tab_q, tab_k, tab_i)


GDN_TC = 512
GDN_NCH = GDN_TC // CHUNK
HALO = 8


def _gdn_kernel(alog_ref, dtb_ref, q_ref, k_ref, v_ref, z_ref, cwq_ref, cwk_ref, cwv_ref,
                ga_ref, gb_ref, ng_ref, o_ref, s_ref, xbuf_ref, carry_ref):
    h = pl.program_id(1)
    n = pl.program_id(2)

    @pl.when(n == 0)
    def _():
        s_ref[...] = jnp.zeros_like(s_ref)
        carry_ref[...] = jnp.zeros_like(carry_ref)

    def conv_silu(x_ref, w_ref, idx):
        x = x_ref[...].astype(F32)
        xbuf_ref[0:HALO, :] = carry_ref[idx]
        xbuf_ref[HALO:HALO + GDN_TC, :] = x
        carry_ref[idx] = x[GDN_TC - HALO:GDN_TC, :]
        w = w_ref[...]
        y = x * w[CONV_WIDTH - 1:CONV_WIDTH, :]
        for d in range(1, CONV_WIDTH):
            y = y + xbuf_ref[HALO - d:HALO - d + GDN_TC, :] * w[CONV_WIDTH - 1 - d:CONV_WIDTH - d, :]
        return y * _sigmoid(y)

    q = conv_silu(q_ref, cwq_ref, 0)
    k = conv_silu(k_ref, cwk_ref, 1)
    v = conv_silu(v_ref, cwv_ref, 2)
    qn = q * lax.rsqrt(jnp.sum(q * q, axis=-1, keepdims=True) + EPS) * (GDN_DK ** -0.5)
    kn = k * lax.rsqrt(jnp.sum(k * k, axis=-1, keepdims=True) + EPS)

    a_neg = -jnp.exp(jnp.full((GDN_NCH, LANES), alog_ref[h], F32))
    g = a_neg * _softplus(ga_ref[...] + dtb_ref[h])
    lane = lax.broadcasted_iota(I32, (GDN_NCH, LANES), 1)
    gc = g
    sh = 1
    while sh < CHUNK:
        gc = gc + jnp.where(lane >= sh, pltpu.roll(gc, sh, 1), 0.0)
        sh *= 2
    beta = _sigmoid(gb_ref[...])
    zpad = jnp.zeros((LANES - GDN_NCH, LANES), F32)
    gc_t = jnp.concatenate([gc, zpad], axis=0).T
    beta_t = jnp.concatenate([beta, zpad], axis=0).T

    ri = lax.broadcasted_iota(I32, (CHUNK, CHUNK), 0)
    ci = lax.broadcasted_iota(I32, (CHUNK, CHUNK), 1)
    incl = ri >= ci
    strict = ri > ci
    eye = jnp.where(ri == ci, 1.0, 0.0).astype(F32)
    ng = ng_ref[...]

    pre = []
    for c in range(GDN_NCH):
        rows = slice(c * CHUNK, (c + 1) * CHUNK)
        qc, kc, vc = qn[rows], kn[rows], v[rows]
        g_row = gc[c:c + 1, 0:CHUNK]
        g_col = jnp.broadcast_to(gc_t[0:CHUNK, c:c + 1], (CHUNK, LANES))
        b_col = jnp.broadcast_to(beta_t[0:CHUNK, c:c + 1], (CHUNK, LANES))
        diff = g_col[:, 0:CHUNK] - g_row
        decay = jnp.where(incl, jnp.exp(jnp.where(incl, diff, 0.0)), 0.0)
        kb16 = kc.astype(BF16)
        kk = _dot_nt(kb16, kb16)
        lmat = jnp.where(strict, b_col[:, 0:CHUNK] * kk * decay, 0.0)
        xp = -lmat
        tm = eye + xp
        for _ in range(5):
            xp16 = xp.astype(BF16)
            xp = _dot(xp16, xp16)
            tm = tm + _dot(tm.astype(BF16), xp.astype(BF16))
        tm16 = tm.astype(BF16)
        eg = jnp.exp(g_col)
        w = _dot(tm16, (kc * (b_col * eg)).astype(BF16))
        u = _dot(tm16, (vc * b_col).astype(BF16))
        a_intra = _dot_nt(qc.astype(BF16), kb16) * decay
        g_last = g_col[CHUNK - 1:CHUNK, :]
        pre.append(dict(
            w=w.astype(BF16), u=u, a=a_intra.astype(BF16), qg=(qc * eg).astype(BF16),
            kd=(kc * jnp.exp(g_last - g_col)).astype(BF16), el=jnp.exp(g_last)))

    s = s_ref[...]
    for c in range(GDN_NCH):
        p = pre[c]
        s16 = s.astype(BF16)
        v_new = p["u"] - _dot(p["w"], s16)
        v16 = v_new.astype(BF16)
        o = _dot(p["qg"], s16) + _dot(p["a"], v16)
        s = s * p["el"] + _dot_tn(p["kd"], v16)
        o = o * lax.rsqrt(jnp.mean(o * o, axis=-1, keepdims=True) + EPS) * ng
        z = z_ref[c * CHUNK:(c + 1) * CHUNK, :].astype(F32)
        o_ref[c * CHUNK:(c + 1) * CHUNK, :] = (o * (z * _sigmoid(z))).astype(BF16)
    s_ref[...] = s


def _gdn(main, conv_w, a_log, dt_bias, ga_r, gb_r, norm_g, batch, seq):
    nt = seq // GDN_TC

    def tok(col0):
        return pl.BlockSpec((GDN_TC, LANES), lambda b, h, n, c=col0 // LANES: (b * nt + n, c + h))

    def cw(col0):
        return pl.BlockSpec((CONV_WIDTH, LANES), lambda b, h, n, c=col0 // LANES: (0, c + h))

    row_spec = pl.BlockSpec((None, None, GDN_NCH, LANES), lambda b, h, n: (b, h, n, 0))
    smem = pl.BlockSpec(memory_space=pltpu.SMEM)
    return pl.pallas_call(
        _gdn_kernel,
        grid=(batch, GDN_HEADS, nt),
        in_specs=[smem, smem, tok(OFF_GQ), tok(OFF_GK), tok(OFF_GV), tok(OFF_GZ),
                  cw(0), cw(GDN_HEADS * GDN_DK), cw(2 * GDN_HEADS * GDN_DK),
                  row_spec, row_spec,
                  pl.BlockSpec((1, LANES), lambda b, h, n: (0, 0))],
        out_specs=pl.BlockSpec((GDN_TC, LANES), lambda b, h, n: (b * nt + n, h)),
        out_shape=jax.ShapeDtypeStruct((batch * seq, GDN_HEADS * GDN_DV), BF16),
        scratch_shapes=[pltpu.VMEM((GDN_DK, GDN_DV), F32),
                        pltpu.VMEM((HALO + GDN_TC, LANES), F32),
                        pltpu.VMEM((3, HALO, LANES), F32)],
        compiler_params=pltpu.CompilerParams(
            dimension_semantics=("arbitrary", "arbitrary", "arbitrary"),
            vmem_limit_bytes=VMEM_LIMIT),
        name="gdn",
    )(a_log, dt_bias, main, main, main, main, conv_w, conv_w, conv_w, ga_r, gb_r, norm_g)


DSA_TQ = 256
DSA_TK = 256
REP = DSA_HEADS // DSA_KV_HEADS


def _dsa_kernel(q_ref, qi_ref, wt_ref, k_ref, vt_ref, ki_ref, z_ref, o_ref,
                keys_ref, qih_ref, acc_ref, m_ref, l_ref, *, k_sel):
    i = pl.program_id(1)
    nkb = i + 1

    for hh in range(IDX_HEADS):
        qih_ref[hh] = qi_ref[:, hh * IDX_DIM:(hh + 1) * IDX_DIM]
    wt = wt_ref[...] * (IDX_HEADS ** -0.5 * IDX_DIM ** -0.5)
    qpos = i * DSA_TQ + lax.broadcasted_iota(I32, (DSA_TK, DSA_TQ), 1)
    krow = lax.broadcasted_iota(I32, (DSA_TK, DSA_TQ), 0)

    def p1(kb, carry):
        off = pl.multiple_of(kb * DSA_TK, DSA_TK)
        kblk = ki_ref[pl.ds(off, DSA_TK), :]
        acc = jnp.zeros((DSA_TK, DSA_TQ), F32)
        for hh in range(IDX_HEADS):
            p = _dot_nt(kblk, qih_ref[hh])
            acc = acc + wt[hh:hh + 1, :] * jnp.maximum(p, 0.0)
        keys_ref[pl.ds(off, DSA_TK), :] = jnp.where(kb * DSA_TK + krow <= qpos, acc, -jnp.inf)
        return carry

    lax.fori_loop(0, nkb, p1, 0)

    def key_to_f32(key):
        return pltpu.bitcast(key ^ ((key >> 31) & 0x7FFFFFFF), F32)

    def p2(p, tau):
        cand = tau ^ lax.shift_left(jnp.int32(1), 31 - p)
        cand_f = key_to_f32(cand)

        def cnt_body(kb, cnt):
            off = pl.multiple_of(kb * DSA_TK, DSA_TK)
            hit = jnp.where(keys_ref[pl.ds(off, DSA_TK), :] >= cand_f, 1, 0)
            return cnt + jnp.sum(hit, axis=0, keepdims=True)

        cnt = lax.fori_loop(0, nkb, cnt_body, jnp.zeros((1, DSA_TQ), I32))
        return jnp.where(cnt >= k_sel, cand, tau)

    tau = lax.fori_loop(0, 32, p2, jnp.full((1, DSA_TQ), INT_MIN, I32))
    tau = key_to_f32(jnp.maximum(tau, KEY_F32_LOWEST))

    m_ref[...] = jnp.full_like(m_ref, NEG_BIG)
    l_ref[...] = jnp.zeros_like(l_ref)
    acc_ref[...] = jnp.zeros_like(acc_ref)

    def p3(kb, carry):
        off = pl.multiple_of(kb * DSA_TK, DSA_TK)
        bias = jnp.where(keys_ref[pl.ds(off, DSA_TK), :] >= tau, 0.0, NEG_BIG)
        for g in range(DSA_KV_HEADS):
            kblk = k_ref[pl.ds(off, DSA_TK), g * DSA_HEAD_DIM:(g + 1) * DSA_HEAD_DIM]
            vblk = vt_ref[g * DSA_HEAD_DIM:(g + 1) * DSA_HEAD_DIM, pl.ds(off, DSA_TK)]
            for r in range(REP):
                hh = g * REP + r
                qh = q_ref[:, hh * DSA_HEAD_DIM:(hh + 1) * DSA_HEAD_DIM]
                s = _dot_nt(kblk, qh) + bias
                m_old = m_ref[hh:hh + 1, :]
                m_new = jnp.maximum(m_old, jnp.max(s, axis=0, keepdims=True))
                alpha = jnp.exp2(m_old - m_new)
                p = jnp.exp2(s - m_new)
                l_ref[hh:hh + 1, :] = alpha * l_ref[hh:hh + 1, :] + jnp.sum(p, axis=0, keepdims=True)
                acc_ref[hh] = alpha * acc_ref[hh] + _dot(vblk, p.astype(BF16))
                m_ref[hh:hh + 1, :] = m_new
        return carry

    lax.fori_loop(0, nkb, p3, 0)

    for hh in range(DSA_HEADS):
        o_t = acc_ref[hh] / l_ref[hh:hh + 1, :]
        z = z_ref[:, hh * DSA_HEAD_DIM:(hh + 1) * DSA_HEAD_DIM].astype(F32)
        o_ref[:, hh * DSA_HEAD_DIM:(hh + 1) * DSA_HEAD_DIM] = (
            o_t.T * (z * _sigmoid(z))).astype(BF16)


def _dsa(main, w_t, v_t, k_idx, batch, seq):
    nq = seq // DSA_TQ
    k_sel = min(TOPK_MAX, seq // 4)
    width = DSA_HEADS * DSA_HEAD_DIM
    kvw = DSA_KV_HEADS * DSA_HEAD_DIM

    def tokw(col0):
        return pl.BlockSpec((DSA_TQ, width), lambda b, i, c=col0 // width: (b * nq + i, c))

    return pl.pallas_call(
        functools.partial(_dsa_kernel, k_sel=k_sel),
        grid=(batch, nq),
        in_specs=[
            tokw(OFF_AQ), tokw(OFF_IQ),
            pl.BlockSpec((None, IDX_HEADS, DSA_TQ), lambda b, i: (b, 0, i)),
            pl.BlockSpec((seq, kvw), lambda b, i: (b, OFF_AK // kvw)),
            pl.BlockSpec((None, kvw, seq), lambda b, i: (b, 0, 0)),
            pl.BlockSpec((None, seq, IDX_DIM), lambda b, i: (b, 0, 0)),
            tokw(OFF_AZ),
        ],
        out_specs=pl.BlockSpec((DSA_TQ, width), lambda b, i: (b * nq + i, 0)),
        out_shape=jax.ShapeDtypeStruct((batch * seq, width), BF16),
        scratch_shapes=[
            pltpu.VMEM((seq, DSA_TQ), F32),
            pltpu.VMEM((IDX_HEADS, DSA_TQ, IDX_DIM), BF16),
            pltpu.VMEM((DSA_HEADS, DSA_HEAD_DIM, DSA_TQ), F32),
            pltpu.VMEM((DSA_HEADS, DSA_TQ), F32),
            pltpu.VMEM((DSA_HEADS, DSA_TQ), F32),
        ],
        compiler_params=pltpu.CompilerParams(
            dimension_semantics=("arbitrary", "arbitrary"), vmem_limit_bytes=VMEM_LIMIT),
        name="dsa",
    )(main, main, w_t, main, v_t, k_idx, main)


OUT_TM = 512


def _out_kernel(oa_ref, ob_ref, x_ref, w_ref, g_ref, out_ref):
    wa = oa_ref.shape[1]
    mixed = _dot(oa_ref[...], w_ref[0:wa, :]) + _dot(ob_ref[...], w_ref[wa:, :])
    y = x_ref[...] + mixed
    ms = jnp.mean(y * y, axis=-1, keepdims=True)
    out_ref[...] = y * lax.rsqrt(ms + EPS) * g_ref[...]


def _output_projection(o_a, o_b, xf, w_out, g):
    n = xf.shape[0]
    wa, wb = o_a.shape[1], o_b.shape[1]
    return pl.pallas_call(
        _out_kernel,
        grid=(n // OUT_TM,),
        in_specs=[
            pl.BlockSpec((OUT_TM, wa), lambda i: (i, 0)),
            pl.BlockSpec((OUT_TM, wb), lambda i: (i, 0)),
            pl.BlockSpec((OUT_TM, D_MODEL), lambda i: (i, 0)),
            pl.BlockSpec((wa + wb, D_MODEL), lambda i: (0, 0)),
            pl.BlockSpec((1, D_MODEL), lambda i: (0, 0)),
        ],
        out_specs=pl.BlockSpec((OUT_TM, D_MODEL), lambda i: (i, 0)),
        out_shape=jax.ShapeDtypeStruct((n, D_MODEL), F32),
        compiler_params=pltpu.CompilerParams(
            dimension_semantics=("arbitrary",), vmem_limit_bytes=VMEM_LIMIT),
        name="out_proj",
    )(o_a, o_b, xf, w_out, g)


def _rope_tables(positions, head_dim, scale):
    rot = head_dim // ROPE_FRACTION
    half = rot // 2
    inv_freq = ROPE_THETA ** (-(jnp.arange(half, dtype=F32) * 2.0 / rot))
    ang = positions.astype(F32).reshape(-1, 1) * inv_freq
    cos, sin = jnp.cos(ang), jnp.sin(ang)
    n = ang.shape[0]
    rest = head_dim - rot
    c = jnp.concatenate([cos, cos, jnp.ones((n, rest), F32)], axis=-1)
    s1 = jnp.concatenate([jnp.zeros((n, half), F32), sin, jnp.zeros((n, rest), F32)], axis=-1)
    s2 = jnp.concatenate([-sin, jnp.zeros((n, half + rest), F32)], axis=-1)
    tab = jnp.stack([c, s1, s2]) * scale
    return jnp.tile(tab, (1, 1, LANES // head_dim))


def _split_w_in(w):
    sizes = (1024, 1024, 1024, 1024, 8, 8, 1024, 256, 256, 1024, 1024, 64, 16)
    offs = [0]
    for s in sizes:
        offs.append(offs[-1] + s)
    parts = [w[:, offs[i]:offs[i + 1]] for i in range(len(sizes))]
    gq, gk, gv, gz, ga, gb, aq, ak, av, az, iq, ik, iw = parts
    w_main = jnp.concatenate([gq, gk, gv, gz, aq, az, iq, ak, av], axis=1).astype(BF16)
    pad = jnp.zeros((w.shape[0], LANES - 96), w.dtype)
    w_small = jnp.concatenate([ik, ga, gb, iw, pad], axis=1).astype(BF16)
    return w_main, w_small


def kernel(x, positions, attn_norm_g, w_in, gdn_conv_w, gdn_a_log, gdn_dt_bias, gdn_norm_g,
           w_out, final_norm_g):
    batch, seq, d = x.shape
    assert d == D_MODEL and w_in.shape[0] == 1, "single-layer trunk with D_MODEL=2048 only"
    assert seq % max(GDN_TC, DSA_TQ) == 0 and (batch * seq) % max(PROJ_TM, OUT_TM) == 0
    xf = x.reshape(batch * seq, d)

    w_main, w_small = _split_w_in(w_in[0])
    q_scale = DSA_HEAD_DIM ** -0.5 * math.log2(math.e)
    tab_q = _rope_tables(positions, DSA_HEAD_DIM, q_scale)
    tab_k = _rope_tables(positions, DSA_HEAD_DIM, 1.0)
    tab_i = _rope_tables(positions, IDX_DIM, 1.0)

    main, small = _input_projection(xf, attn_norm_g[0].reshape(1, d), w_main, w_small,
                                    tab_q, tab_k, tab_i)

    def rows(col0, width):
        a = small[:, col0:col0 + width].reshape(batch, seq, width)
        return jnp.swapaxes(a, 1, 2)

    def chunk_rows(col0):
        a = rows(col0, GDN_HEADS).reshape(batch, GDN_HEADS, seq // CHUNK, CHUNK)
        return jnp.pad(a, ((0, 0), (0, 0), (0, 0), (0, LANES - CHUNK)))

    o_a = _gdn(main, gdn_conv_w[0], gdn_a_log[0], gdn_dt_bias[0],
               chunk_rows(SM_GA), chunk_rows(SM_GB), gdn_norm_g[0].reshape(1, GDN_DV),
               batch, seq)

    k_idx = small[:, SM_IK:SM_IK + IDX_DIM].astype(BF16).reshape(batch, seq, IDX_DIM)
    w_t = rows(SM_IW, IDX_HEADS)
    kvw = DSA_KV_HEADS * DSA_HEAD_DIM
    v_t = jnp.swapaxes(main[:, OFF_AV:OFF_AV + kvw].reshape(batch, seq, kvw), 1, 2)
    o_b = _dsa(main, w_t, v_t, k_idx, batch, seq)

    out = _output_projection(o_a, o_b, xf, w_out[0].astype(BF16), final_norm_g.reshape(1, d))
    return out.reshape(batch, seq, d)
```

```python
import functools
import math

import jax
import jax.numpy as jnp
from jax import lax
from jax.experimental import pallas as pl
from jax.experimental.pallas import tpu as pltpu

F32 = jnp.float32
BF16 = jnp.bfloat16
I32 = jnp.int32

D_MODEL = 2048
GDN_HEADS = 8
GDN_DK = 128
GDN_DV = 128
CONV_WIDTH = 4
CHUNK = 64
DSA_HEADS = 8
DSA_KV_HEADS = 2
DSA_HEAD_DIM = 128
IDX_HEADS = 16
IDX_DIM = 64
TOPK_MAX = 256
ROPE_THETA = 500000.0
ROPE_FRACTION = 4
EPS = 1e-6

LANES = 128
SUBLANES = 8
VMEM_LIMIT = 56 * 1024 * 1024

OFF_GQ, OFF_GK, OFF_GV, OFF_GZ = 0, 1024, 2048, 3072
OFF_AQ, OFF_AZ, OFF_IQ, OFF_AK, OFF_AV = 4096, 5120, 6144, 7168, 7424
MAIN_COLS = 7680
SM_IK, SM_GA, SM_GB, SM_IW = 0, 64, 72, 80

INT_MIN = -2147483648
KEY_F32_LOWEST = INT_MIN + 0x00800000
NEG_BIG = -1e30


def _sigmoid(x):
    return 1.0 / (1.0 + jnp.exp(-x))


def _softplus(x):
    return jnp.maximum(x, 0.0) + jnp.log(1.0 + jnp.exp(-jnp.abs(x)))


def _dot(a, b):
    return jnp.dot(a, b, preferred_element_type=F32)


def _dot_nt(a, b):
    return lax.dot_general(a, b, (((1,), (1,)), ((), ())), preferred_element_type=F32)


def _dot_tn(a, b):
    return lax.dot_general(a, b, (((0,), (0,)), ((), ())), preferred_element_type=F32)


PROJ_TM = 512
PROJ_TN = 512


def _rope_slab(xs, tab_ref, half):
    n = xs.shape[-1]
    return (xs * tab_ref[0] + pltpu.roll(xs, half, 1) * tab_ref[1]
            + pltpu.roll(xs, n - half, 1) * tab_ref[2])


def _proj_kernel(x_ref, g_ref, w_ref, ws_ref, tq_ref, tk_ref, ti_ref, main_ref, small_ref, h_ref):
    j = pl.program_id(1)

    @pl.when(j == 0)
    def _():
        x = x_ref[...]
        ms = jnp.mean(x * x, axis=-1, keepdims=True)
        h = (x * lax.rsqrt(ms + EPS) * g_ref[...]).astype(BF16)
        h_ref[...] = h
        s = _dot(h, ws_ref[...])
        roped = _rope_slab(s, ti_ref, IDX_DIM // ROPE_FRACTION // 2)
        lane = lax.broadcasted_iota(I32, s.shape, 1)
        small_ref[...] = jnp.where(lane < IDX_DIM, roped, s)

    acc = _dot(h_ref[...], w_ref[...])
    nslab = PROJ_TN // LANES
    jq0 = OFF_AQ // PROJ_TN
    ji0 = OFF_IQ // PROJ_TN
    jk = OFF_AK // PROJ_TN
    is_q = (j == jq0) | (j == jq0 + 1)
    is_i = (j == ji0) | (j == ji0 + 1)
    is_k = j == jk

    def slab(c):
        return acc[:, c * LANES:(c + 1) * LANES]

    @pl.when(is_q)
    def _():
        for c in range(nslab):
            main_ref[:, c * LANES:(c + 1) * LANES] = _rope_slab(
                slab(c), tq_ref, DSA_HEAD_DIM // ROPE_FRACTION // 2).astype(BF16)

    @pl.when(is_i)
    def _():
        for c in range(nslab):
            main_ref[:, c * LANES:(c + 1) * LANES] = _rope_slab(
                slab(c), ti_ref, IDX_DIM // ROPE_FRACTION // 2).astype(BF16)

    @pl.when(is_k)
    def _():
        for c in range(nslab):
            if c < DSA_KV_HEADS:
                val = _rope_slab(slab(c), tk_ref, DSA_HEAD_DIM // ROPE_FRACTION // 2)
            else:
                val = slab(c)
            main_ref[:, c * LANES:(c + 1) * LANES] = val.astype(BF16)

    @pl.when(jnp.logical_not(is_q | is_i | is_k))
    def _():
        main_ref[...] = acc.astype(BF16)


def _input_projection(xf, g, w_main, w_small, tab_q, tab_k, tab_i):
    n = xf.shape[0]
    grid = (n // PROJ_TM, MAIN_COLS // PROJ_TN)
    tab_spec = pl.BlockSpec((3, PROJ_TM, LANES), lambda i, j: (0, i, 0))
    return pl.pallas_call(
        _proj_kernel,
        grid=grid,
        in_specs=[
            pl.BlockSpec((PROJ_TM, D_MODEL), lambda i, j: (i, 0)),
            pl.BlockSpec((1, D_MODEL), lambda i, j: (0, 0)),
            pl.BlockSpec((D_MODEL, PROJ_TN), lambda i, j: (0, j)),
            pl.BlockSpec((D_MODEL, LANES), lambda i, j: (0, 0)),
            tab_spec, tab_spec, tab_spec,
        ],
        out_specs=[
            pl.BlockSpec((PROJ_TM, PROJ_TN), lambda i, j: (i, j)),
            pl.BlockSpec((PROJ_TM, LANES), lambda i, j: (i, 0)),
        ],
        out_shape=[
            jax.ShapeDtypeStruct((n, MAIN_COLS), BF16),
            jax.ShapeDtypeStruct((n, LANES), F32),
        ],
        scratch_shapes=[pltpu.VMEM((PROJ_TM, D_MODEL), BF16)],
        compiler_params=pltpu.CompilerParams(
            dimension_semantics=("arbitrary", "arbitrary"), vmem_limit_bytes=VMEM_LIMIT),
        name="in_proj",
    )(xf, g, w_main, w_small, tab_q, tab_k, tab_i)


GDN_TC = 256
GDN_NCH = GDN_TC // CHUNK
GDN_HB = 4
GDN_R = GDN_HB * GDN_NCH
HALO = 8


def _gdn_kernel(q_ref, k_ref, v_ref, z_ref, cwq_ref, cwk_ref, cwv_ref, ga_ref, gb_ref,
                alog_ref, dtb_ref, ng_ref, o_ref, s_ref, xbuf_ref, carry_ref):
    n = pl.program_id(2)

    @pl.when(n == 0)
    def _():
        s_ref[...] = jnp.zeros_like(s_ref)
        carry_ref[...] = jnp.zeros_like(carry_ref)

    def conv_silu(x_ref, w_ref, idx):
        x = x_ref[...].astype(F32)
        xbuf_ref[idx, 0:HALO, :] = carry_ref[idx]
        xbuf_ref[idx, HALO:HALO + GDN_TC, :] = x
        carry_ref[idx] = x[GDN_TC - HALO:GDN_TC, :]
        w = w_ref[...]
        y = x * w[CONV_WIDTH - 1:CONV_WIDTH, :]
        for d in range(1, CONV_WIDTH):
            y = y + (xbuf_ref[idx, HALO - d:HALO - d + GDN_TC, :]
                     * w[CONV_WIDTH - 1 - d:CONV_WIDTH - d, :])
        return y * _sigmoid(y)

    q = conv_silu(q_ref, cwq_ref, 0)
    k = conv_silu(k_ref, cwk_ref, 1)
    v = conv_silu(v_ref, cwv_ref, 2)

    def l2n(x):
        return x * lax.rsqrt(jnp.sum(x * x, axis=-1, keepdims=True) + EPS)

    def head(x, hl):
        return x[:, hl * LANES:(hl + 1) * LANES]

    qn = [l2n(head(q, hl)) * (GDN_DK ** -0.5) for hl in range(GDN_HB)]
    kn = [l2n(head(k, hl)) for hl in range(GDN_HB)]

    g = -jnp.exp(alog_ref[...]) * _softplus(ga_ref[...] + dtb_ref[...])
    lane = lax.broadcasted_iota(I32, (GDN_R, LANES), 1)
    gc = g
    sh = 1
    while sh < CHUNK:
        gc = gc + jnp.where(lane >= sh, pltpu.roll(gc, sh, 1), 0.0)
        sh *= 2
    beta = _sigmoid(gb_ref[...])
    zpad = jnp.zeros((LANES - GDN_R, LANES), F32)
    gc_t = jnp.concatenate([gc, zpad], axis=0).T
    beta_t = jnp.concatenate([beta, zpad], axis=0).T

    ri = lax.broadcasted_iota(I32, (CHUNK, CHUNK), 0)
    ci = lax.broadcasted_iota(I32, (CHUNK, CHUNK), 1)
    incl = ri >= ci
    strict = ri > ci
    eye = jnp.where(ri == ci, 1.0, 0.0).astype(F32)
    ng = ng_ref[...]
    rs = range(GDN_R)

    qc, kc, vc, g_col, b_col, decay, eg = [], [], [], [], [], [], []
    for r in rs:
        hl, c = divmod(r, GDN_NCH)
        rows = slice(c * CHUNK, (c + 1) * CHUNK)
        qc.append(qn[hl][rows])
        kc.append(kn[hl][rows])
        vc.append(head(v, hl)[rows])
        g_col.append(jnp.broadcast_to(gc_t[0:CHUNK, r:r + 1], (CHUNK, LANES)))
        b_col.append(jnp.broadcast_to(beta_t[0:CHUNK, r:r + 1], (CHUNK, LANES)))
        diff = g_col[r][:, 0:CHUNK] - gc[r:r + 1, 0:CHUNK]
        decay.append(jnp.where(incl, jnp.exp(jnp.where(incl, diff, 0.0)), 0.0))
        eg.append(jnp.exp(g_col[r]))

    qk = [_dot_nt(jnp.concatenate([qc[r], kc[r]], axis=0).astype(BF16), kc[r].astype(BF16))
          for r in rs]

    xk = [-jnp.where(strict, b_col[r][:, 0:CHUNK] * qk[r][CHUNK:] * decay[r], 0.0) for r in rs]
    tm = [eye + xk[r] for r in rs]
    xk = [_dot(xk[r].astype(BF16), xk[r].astype(BF16)) for r in rs]
    for _ in range(4):
        st = [_dot(jnp.concatenate([tm[r], xk[r]], axis=0).astype(BF16), xk[r].astype(BF16))
              for r in rs]
        tm = [tm[r] + st[r][0:CHUNK] for r in rs]
        xk = [st[r][CHUNK:] for r in rs]
    tm = [tm[r] + _dot(tm[r].astype(BF16), xk[r].astype(BF16)) for r in rs]

    wu = [_dot(tm[r].astype(BF16),
               jnp.concatenate([kc[r] * (b_col[r] * eg[r]), vc[r] * b_col[r]], axis=1).astype(BF16)
               ).astype(BF16) for r in rs]
    awu = [_dot((qk[r][0:CHUNK] * decay[r]).astype(BF16), wu[r]) for r in rs]
    g_last = [g_col[r][CHUNK - 1:CHUNK, :] for r in rs]
    mn = [_dot_tn((kc[r] * jnp.exp(g_last[r] - g_col[r])).astype(BF16), wu[r]) for r in rs]
    lhs = [jnp.concatenate([qc[r] * eg[r] - awu[r][:, 0:LANES], mn[r][:, 0:LANES]],
                           axis=0).astype(BF16) for r in rs]
    el = [jnp.exp(g_last[r]) for r in rs]

    s = [s_ref[hl] for hl in range(GDN_HB)]
    outs = {}
    for c in range(GDN_NCH):
        for hl in range(GDN_HB):
            r = hl * GDN_NCH + c
            st = _dot(lhs[r], s[hl].astype(BF16))
            outs[r] = st[0:CHUNK] + awu[r][:, LANES:]
            s[hl] = s[hl] * el[r] - st[CHUNK:] + mn[r][:, LANES:]
    for hl in range(GDN_HB):
        s_ref[hl] = s[hl]
    for r in rs:
        hl, c = divmod(r, GDN_NCH)
        o = outs[r]
        o = o * lax.rsqrt(jnp.mean(o * o, axis=-1, keepdims=True) + EPS) * ng
        z = z_ref[c * CHUNK:(c + 1) * CHUNK, hl * LANES:(hl + 1) * LANES].astype(F32)
        o_ref[c * CHUNK:(c + 1) * CHUNK, hl * LANES:(hl + 1) * LANES] = (
            o * (z * _sigmoid(z))).astype(BF16)


def _gdn(main, conv_w, alog_rows, dtb_rows, ga_r, gb_r, norm_g, batch, seq):
    nt = seq // GDN_TC
    hw = GDN_HB * LANES

    def tok(col0):
        return pl.BlockSpec((GDN_TC, hw), lambda b, hg, n, c=col0 // hw: (b * nt + n, c + hg))

    def cw(col0):
        return pl.BlockSpec((CONV_WIDTH, hw), lambda b, hg, n, c=col0 // hw: (0, c + hg))

    row_spec = pl.BlockSpec((None, None, GDN_R, LANES), lambda b, hg, n: (b, n, hg, 0))
    par_spec = pl.BlockSpec((GDN_R, LANES), lambda b, hg, n: (hg, 0))
    return pl.pallas_call(
        _gdn_kernel,
        grid=(batch, GDN_HEADS // GDN_HB, nt),
        in_specs=[tok(OFF_GQ), tok(OFF_GK), tok(OFF_GV), tok(OFF_GZ),
                  cw(0), cw(GDN_HEADS * GDN_DK), cw(2 * GDN_HEADS * GDN_DK),
                  row_spec, row_spec, par_spec, par_spec,
                  pl.BlockSpec((1, LANES), lambda b, hg, n: (0, 0))],
        out_specs=pl.BlockSpec((GDN_TC, hw), lambda b, hg, n: (b * nt + n, hg)),
        out_shape=jax.ShapeDtypeStruct((batch * seq, GDN_HEADS * GDN_DV), BF16),
        scratch_shapes=[pltpu.VMEM((GDN_HB, GDN_DK, GDN_DV), F32),
                        pltpu.VMEM((3, HALO + GDN_TC, hw), F32),
                        pltpu.VMEM((3, HALO, hw), F32)],
        compiler_params=pltpu.CompilerParams(
            dimension_semantics=("arbitrary", "arbitrary", "arbitrary"),
            vmem_limit_bytes=VMEM_LIMIT),
        name="gdn",
    )(main, main, main, main, conv_w, conv_w, conv_w, ga_r, gb_r, alog_rows, dtb_rows, norm_g)


DSA_TQ = 256
DSA_TK = 256
REP = DSA_HEADS // DSA_KV_HEADS


def _dsa_kernel(q_ref, qi_ref, wt_ref, k_ref, vt_ref, ki_ref, z_ref, o_ref,
                keys_ref, qih_ref, acc_ref, m_ref, l_ref, *, k_sel):
    i = pl.program_id(1)
    nkb = i + 1

    for hh in range(IDX_HEADS):
        qih_ref[hh] = qi_ref[:, hh * IDX_DIM:(hh + 1) * IDX_DIM]
    wt = wt_ref[...] * (IDX_HEADS ** -0.5 * IDX_DIM ** -0.5)
    qpos = i * DSA_TQ + lax.broadcasted_iota(I32, (DSA_TK, DSA_TQ), 1)
    krow = lax.broadcasted_iota(I32, (DSA_TK, DSA_TQ), 0)

    def p1(kb, carry):
        off = pl.multiple_of(kb * DSA_TK, DSA_TK)
        kblk = ki_ref[pl.ds(off, DSA_TK), :]
        acc = jnp.zeros((DSA_TK, DSA_TQ), F32)
        for hh in range(IDX_HEADS):
            p = _dot_nt(kblk, qih_ref[hh])
            acc = acc + wt[hh:hh + 1, :] * jnp.maximum(p, 0.0)
        keys_ref[pl.ds(off, DSA_TK), :] = jnp.where(kb * DSA_TK + krow <= qpos, acc, -jnp.inf)
        return carry

    lax.fori_loop(0, nkb, p1, 0)

    def key_to_f32(key):
        return pltpu.bitcast(key ^ ((key >> 31) & 0x7FFFFFFF), F32)

    def p2(p, tau):
        cand = tau ^ lax.shift_left(jnp.int32(1), 31 - p)
        cand_f = key_to_f32(cand)

        def cnt_body(kb, cnt):
            off = pl.multiple_of(kb * DSA_TK, DSA_TK)
            hit = jnp.where(keys_ref[pl.ds(off, DSA_TK), :] >= cand_f, 1, 0)
            parts = [hit[r:r + SUBLANES, :] for r in range(0, DSA_TK, SUBLANES)]
            while len(parts) > 1:
                parts = [a + b for a, b in zip(parts[0::2], parts[1::2])]
            return cnt + parts[0]

        cnt = lax.fori_loop(0, nkb, cnt_body, jnp.zeros((SUBLANES, DSA_TQ), I32))
        cnt = jnp.sum(cnt, axis=0, keepdims=True)
        return jnp.where(cnt >= k_sel, cand, tau)

    tau = lax.fori_loop(0, 32, p2, jnp.full((1, DSA_TQ), INT_MIN, I32))
    tau = key_to_f32(jnp.maximum(tau, KEY_F32_LOWEST))

    m_ref[...] = jnp.full_like(m_ref, NEG_BIG)
    l_ref[...] = jnp.zeros_like(l_ref)
    acc_ref[...] = jnp.zeros_like(acc_ref)

    def p3(kb, carry):
        off = pl.multiple_of(kb * DSA_TK, DSA_TK)
        bias = jnp.where(keys_ref[pl.ds(off, DSA_TK), :] >= tau, 0.0, NEG_BIG)
        heads = range(DSA_HEADS)
        kblk = [k_ref[pl.ds(off, DSA_TK), g * DSA_HEAD_DIM:(g + 1) * DSA_HEAD_DIM]
                for g in range(DSA_KV_HEADS)]
        vblk = [vt_ref[g * DSA_HEAD_DIM:(g + 1) * DSA_HEAD_DIM, pl.ds(off, DSA_TK)]
                for g in range(DSA_KV_HEADS)]
        s = [_dot_nt(kblk[hh // REP], q_ref[:, hh * DSA_HEAD_DIM:(hh + 1) * DSA_HEAD_DIM]) + bias
             for hh in heads]
        m_old = [m_ref[hh] for hh in heads]
        m_new = [jnp.maximum(m_old[hh], jnp.max(s[hh], axis=0, keepdims=True)) for hh in heads]
        alpha = [jnp.exp2(m_old[hh] - m_new[hh]) for hh in heads]
        p = [jnp.exp2(s[hh] - m_new[hh][0:1, :]) for hh in heads]
        for hh in heads:
            l_ref[hh] = alpha[hh] * l_ref[hh] + jnp.sum(p[hh], axis=0, keepdims=True)
            m_ref[hh] = m_new[hh]
        pv = [_dot(vblk[hh // REP], p[hh].astype(BF16)) for hh in heads]
        for hh in heads:
            acc_ref[hh] = alpha[hh][0:1, :] * acc_ref[hh] + pv[hh]
        return carry

    lax.fori_loop(0, nkb, p3, 0)

    for hh in range(DSA_HEADS):
        o_t = acc_ref[hh] / l_ref[hh][0:1, :]
        z = z_ref[:, hh * DSA_HEAD_DIM:(hh + 1) * DSA_HEAD_DIM].astype(F32)
        o_ref[:, hh * DSA_HEAD_DIM:(hh + 1) * DSA_HEAD_DIM] = (
            o_t.T * (z * _sigmoid(z))).astype(BF16)


def _dsa(main, w_t, v_t, k_idx, batch, seq):
    nq = seq // DSA_TQ
    k_sel = min(TOPK_MAX, seq // 4)
    width = DSA_HEADS * DSA_HEAD_DIM
    kvw = DSA_KV_HEADS * DSA_HEAD_DIM

    def tokw(col0):
        return pl.BlockSpec((DSA_TQ, width), lambda b, i, c=col0 // width: (b * nq + i, c))

    return pl.pallas_call(
        functools.partial(_dsa_kernel, k_sel=k_sel),
        grid=(batch, nq),
        in_specs=[
            tokw(OFF_AQ), tokw(OFF_IQ),
            pl.BlockSpec((None, IDX_HEADS, DSA_TQ), lambda b, i: (b, 0, i)),
            pl.BlockSpec((seq, kvw), lambda b, i: (b, OFF_AK // kvw)),
            pl.BlockSpec((None, kvw, seq), lambda b, i: (b, 0, 0)),
            pl.BlockSpec((None, seq, IDX_DIM), lambda b, i: (b, 0, 0)),
            tokw(OFF_AZ),
        ],
        out_specs=pl.BlockSpec((DSA_TQ, width), lambda b, i: (b * nq + i, 0)),
        out_shape=jax.ShapeDtypeStruct((batch * seq, width), BF16),
        scratch_shapes=[
            pltpu.VMEM((seq, DSA_TQ), F32),
            pltpu.VMEM((IDX_HEADS, DSA_TQ, IDX_DIM), BF16),
            pltpu.VMEM((DSA_HEADS, DSA_HEAD_DIM, DSA_TQ), F32),
            pltpu.VMEM((DSA_HEADS, SUBLANES, DSA_TQ), F32),
            pltpu.VMEM((DSA_HEADS, SUBLANES, DSA_TQ), F32),
        ],
        compiler_params=pltpu.CompilerParams(
            dimension_semantics=("arbitrary", "arbitrary"), vmem_limit_bytes=VMEM_LIMIT),
        name="dsa",
    )(main, main, w_t, main, v_t, k_idx, main)


OUT_TM = 512


def _out_kernel(oa_ref, ob_ref, x_ref, w_ref, g_ref, out_ref):
    wa = oa_ref.shape[1]
    mixed = _dot(oa_ref[...], w_ref[0:wa, :]) + _dot(ob_ref[...], w_ref[wa:, :])
    y = x_ref[...] + mixed
    ms = jnp.mean(y * y, axis=-1, keepdims=True)
    out_ref[...] = y * lax.rsqrt(ms + EPS) * g_ref[...]


def _output_projection(o_a, o_b, xf, w_out, g):
    n = xf.shape[0]
    wa, wb = o_a.shape[1], o_b.shape[1]
    return pl.pallas_call(
        _out_kernel,
        grid=(n // OUT_TM,),
        in_specs=[
            pl.BlockSpec((OUT_TM, wa), lambda i: (i, 0)),
            pl.BlockSpec((OUT_TM, wb), lambda i: (i, 0)),
            pl.BlockSpec((OUT_TM, D_MODEL), lambda i: (i, 0)),
            pl.BlockSpec((wa + wb, D_MODEL), lambda i: (0, 0)),
            pl.BlockSpec((1, D_MODEL), lambda i: (0, 0)),
        ],
        out_specs=pl.BlockSpec((OUT_TM, D_MODEL), lambda i: (i, 0)),
        out_shape=jax.ShapeDtypeStruct((n, D_MODEL), F32),
        compiler_params=pltpu.CompilerParams(
            dimension_semantics=("arbitrary",), vmem_limit_bytes=VMEM_LIMIT),
        name="out_proj",
    )(o_a, o_b, xf, w_out, g)


def _rope_tables(positions, head_dim, scale):
    rot = head_dim // ROPE_FRACTION
    half = rot // 2
    inv_freq = ROPE_THETA ** (-(jnp.arange(half, dtype=F32) * 2.0 / rot))
    ang = positions.astype(F32).reshape(-1, 1) * inv_freq
    cos, sin = jnp.cos(ang), jnp.sin(ang)
    n = ang.shape[0]
    rest = head_dim - rot
    c = jnp.concatenate([cos, cos, jnp.ones((n, rest), F32)], axis=-1)
    s1 = jnp.concatenate([jnp.zeros((n, half), F32), sin, jnp.zeros((n, rest), F32)], axis=-1)
    s2 = jnp.concatenate([-sin, jnp.zeros((n, half + rest), F32)], axis=-1)
    tab = jnp.stack([c, s1, s2]) * scale
    return jnp.tile(tab, (1, 1, LANES // head_dim))


def _split_w_in(w):
    sizes = (1024, 1024, 1024, 1024, 8, 8, 1024, 256, 256, 1024, 1024, 64, 16)
    offs = [0]
    for s in sizes:
        offs.append(offs[-1] + s)
    parts = [w[:, offs[i]:offs[i + 1]] for i in range(len(sizes))]
    gq, gk, gv, gz, ga, gb, aq, ak, av, az, iq, ik, iw = parts
    w_main = jnp.concatenate([gq, gk, gv, gz, aq, az, iq, ak, av], axis=1).astype(BF16)
    pad = jnp.zeros((w.shape[0], LANES - 96), w.dtype)
    w_small = jnp.concatenate([ik, ga, gb, iw, pad], axis=1).astype(BF16)
    return w_main, w_small


def kernel(x, positions, attn_norm_g, w_in, gdn_conv_w, gdn_a_log, gdn_dt_bias, gdn_norm_g,
           w_out, final_norm_g):
    batch, seq, d = x.shape
    assert d == D_MODEL and w_in.shape[0] == 1, "single-layer trunk with D_MODEL=2048 only"
    assert seq % max(GDN_TC, DSA_TQ) == 0 and (batch * seq) % max(PROJ_TM, OUT_TM) == 0
    xf = x.reshape(batch * seq, d)

    w_main, w_small = _split_w_in(w_in[0])
    q_scale = DSA_HEAD_DIM ** -0.5 * math.log2(math.e)
    tab_q = _rope_tables(positions, DSA_HEAD_DIM, q_scale)
    tab_k = _rope_tables(positions, DSA_HEAD_DIM, 1.0)
    tab_i = _rope_tables(positions, IDX_DIM, 1.0)

    main, small = _input_projection(xf, attn_norm_g[0].reshape(1, d), w_main, w_small,
                                    tab_q, tab_k, tab_i)

    def rows(col0, width):
        a = small[:, col0:col0 + width].reshape(batch, seq, width)
        return jnp.swapaxes(a, 1, 2)

    def chunk_rows(col0):
        nt = seq // GDN_TC
        a = small[:, col0:col0 + GDN_HEADS].reshape(batch, nt, GDN_NCH, CHUNK, GDN_HEADS)
        a = a.transpose(0, 1, 4, 2, 3).reshape(batch, nt, GDN_HEADS * GDN_NCH, CHUNK)
        return jnp.pad(a, ((0, 0), (0, 0), (0, 0), (0, LANES - CHUNK)))

    def head_rows(p):
        return jnp.broadcast_to(jnp.repeat(p, GDN_NCH)[:, None], (GDN_HEADS * GDN_NCH, LANES))

    o_a = _gdn(main, gdn_conv_w[0], head_rows(gdn_a_log[0]), head_rows(gdn_dt_bias[0]),
               chunk_rows(SM_GA), chunk_rows(SM_GB), gdn_norm_g[0].reshape(1, GDN_DV),
               batch, seq)

    k_idx = small[:, SM_IK:SM_IK + IDX_DIM].astype(BF16).reshape(batch, seq, IDX_DIM)
    w_t = rows(SM_IW, IDX_HEADS)
    kvw = DSA_KV_HEADS * DSA_HEAD_DIM
    v_t = jnp.swapaxes(main[:, OFF_AV:OFF_AV + kvw].reshape(batch, seq, kvw), 1, 2)
    o_b = _dsa(main, w_t, v_t, k_idx, batch, seq)

    out = _output_projection(o_a, o_b, xf, w_out[0].astype(BF16), final_norm_g.reshape(1, d))
    return out.reshape(batch, seq, d)
```

```python
import functools
import math

import jax
import jax.numpy as jnp
from jax import lax
from jax.experimental import pallas as pl
from jax.experimental.pallas import tpu as pltpu

F32 = jnp.float32
BF16 = jnp.bfloat16
I32 = jnp.int32

D_MODEL = 2048
GDN_HEADS = 8
GDN_DK = 128
GDN_DV = 128
CONV_WIDTH = 4
CHUNK = 64
DSA_HEADS = 8
DSA_KV_HEADS = 2
DSA_HEAD_DIM = 128
IDX_HEADS = 16
IDX_DIM = 64
TOPK_MAX = 256
ROPE_THETA = 500000.0
ROPE_FRACTION = 4
EPS = 1e-6

LANES = 128
SUBLANES = 8
VMEM_LIMIT = 56 * 1024 * 1024

OFF_GQ, OFF_GK, OFF_GV, OFF_GZ = 0, 1024, 2048, 3072
OFF_AQ, OFF_AZ, OFF_IQ, OFF_AK, OFF_AV = 4096, 5120, 6144, 7168, 7424
MAIN_COLS = 7680
WA_COLS = 4096
WB_SRC0 = 4112
WB_COLS = 3584
WB_MAP = ((0, 1024, OFF_AQ), (1024, 256, OFF_AK), (1280, 256, OFF_AV),
          (1536, 1024, OFF_AZ), (2560, 1024, OFF_IQ))
SM_IK, SM_IW, SM_GA, SM_GB = 0, 64, 80, 88

INT_MIN = -2147483648
KEY_F32_LOWEST = INT_MIN + 0x00800000
NEG_BIG = -1e30


def _sigmoid(x):
    return 1.0 / (1.0 + jnp.exp(-x))


def _softplus(x):
    return jnp.maximum(x, 0.0) + jnp.log(1.0 + jnp.exp(-jnp.abs(x)))


def _dot(a, b):
    return jnp.dot(a, b, preferred_element_type=F32)


def _dot_nt(a, b):
    return lax.dot_general(a, b, (((1,), (1,)), ((), ())), preferred_element_type=F32)


def _dot_tn(a, b):
    return lax.dot_general(a, b, (((0,), (0,)), ((), ())), preferred_element_type=F32)


PROJ_TM = 256
PROJ_SEG = 512
HALF128 = DSA_HEAD_DIM // ROPE_FRACTION // 2
HALF64 = IDX_DIM // ROPE_FRACTION // 2
Q_SCALE = DSA_HEAD_DIM ** -0.5 * math.log2(math.e)


def _rope_slab(xs, tab_ref, half, head_dim):
    n = xs.shape[-1]
    lane = lax.broadcasted_iota(I32, xs.shape, 1) % head_dim
    partner = jnp.where(lane < half, pltpu.roll(xs, n - half, 1), pltpu.roll(xs, half, 1))
    return xs * tab_ref[0] + partner * tab_ref[1]


def _col_kind(col):
    if OFF_AQ <= col < OFF_AQ + DSA_HEADS * DSA_HEAD_DIM:
        return "q"
    if OFF_AK <= col < OFF_AK + DSA_KV_HEADS * DSA_HEAD_DIM:
        return "k"
    if OFF_IQ <= col < OFF_IQ + IDX_HEADS * IDX_DIM:
        return "i"
    return "plain"


def _wb_dest(col):
    for src0, width, dst0 in WB_MAP:
        if src0 <= col < src0 + width:
            return dst0 + col - src0
    raise ValueError(col)


def _proj_kernel(x_ref, g_ref, wa_ref, wb_ref, ws_ref, t128_ref, t64_ref,
                 main_ref, small_ref, kidx_ref):
    x = x_ref[...]
    ms = jnp.mean(x * x, axis=-1, keepdims=True)
    h = (x * lax.rsqrt(ms + EPS) * g_ref[...]).astype(BF16)
    s = _dot(h, ws_ref[...])
    lane = lax.broadcasted_iota(I32, s.shape, 1)
    s = jnp.where(lane < IDX_DIM, _rope_slab(s, t64_ref, HALF64, IDX_DIM), s)
    small_ref[...] = s
    kidx_ref[...] = s[:, SM_IK:SM_IK + IDX_DIM].astype(BF16)

    def segment(w_ref, c0, dest):
        acc = _dot(h, w_ref[:, c0:c0 + PROJ_SEG])
        for c in range(0, PROJ_SEG, LANES):
            val = acc[:, c:c + LANES]
            d0 = dest(c0 + c)
            kind = _col_kind(d0)
            if kind == "q":
                val = _rope_slab(val, t128_ref, HALF128, DSA_HEAD_DIM) * Q_SCALE
            elif kind == "k":
                val = _rope_slab(val, t128_ref, HALF128, DSA_HEAD_DIM)
            elif kind == "i":
                val = _rope_slab(val, t64_ref, HALF64, IDX_DIM)
            main_ref[:, d0:d0 + LANES] = val.astype(BF16)

    for c0 in range(0, WA_COLS, PROJ_SEG):
        segment(wa_ref, c0, lambda col: col)
    for c0 in range(0, WB_COLS, PROJ_SEG):
        segment(wb_ref, c0, _wb_dest)


def _input_projection(xf, g, w_a, w_b, w_small, tab128, tab64):
    n = xf.shape[0]
    tab_spec = pl.BlockSpec((2, PROJ_TM, LANES), lambda i: (0, i, 0))
    resident = pl.Buffered(1)
    return pl.pallas_call(
        _proj_kernel,
        grid=(n // PROJ_TM,),
        in_specs=[
            pl.BlockSpec((PROJ_TM, D_MODEL), lambda i: (i, 0)),
            pl.BlockSpec((1, D_MODEL), lambda i: (0, 0)),
            pl.BlockSpec((D_MODEL, WA_COLS), lambda i: (0, 0), pipeline_mode=resident),
            pl.BlockSpec((D_MODEL, WB_COLS), lambda i: (0, 0), pipeline_mode=resident),
            pl.BlockSpec((D_MODEL, LANES), lambda i: (0, 0), pipeline_mode=resident),
            tab_spec, tab_spec,
        ],
        out_specs=[
            pl.BlockSpec((PROJ_TM, MAIN_COLS), lambda i: (i, 0)),
            pl.BlockSpec((PROJ_TM, LANES), lambda i: (i, 0)),
            pl.BlockSpec((PROJ_TM, IDX_DIM), lambda i: (i, 0)),
        ],
        out_shape=[
            jax.ShapeDtypeStruct((n, MAIN_COLS), BF16),
            jax.ShapeDtypeStruct((n, LANES), F32),
            jax.ShapeDtypeStruct((n, IDX_DIM), BF16),
        ],
        compiler_params=pltpu.CompilerParams(
            dimension_semantics=("arbitrary",), vmem_limit_bytes=VMEM_LIMIT),
        name="in_proj",
    )(xf, g, w_a, w_b, w_small, tab128, tab64)


GDN_TC = 256
GDN_NCH = GDN_TC // CHUNK
GDN_HB = GDN_HEADS
GDN_R = GDN_HB * GDN_NCH
HALO = 8


def _gdn_kernel(q_ref, k_ref, v_ref, z_ref, cwq_ref, cwk_ref, cwv_ref, sm_ref,
                alog_ref, dtb_ref, ng_ref, o_ref, s_ref, xbuf_ref, carry_ref, rows_ref):
    n = pl.program_id(1)

    @pl.when(n == 0)
    def _():
        s_ref[...] = jnp.zeros_like(s_ref)
        carry_ref[...] = jnp.zeros_like(carry_ref)

    def conv_silu(x_ref, w_ref, idx):
        x = x_ref[...].astype(F32)
        xbuf_ref[idx, 0:HALO, :] = carry_ref[idx]
        xbuf_ref[idx, HALO:HALO + GDN_TC, :] = x
        carry_ref[idx] = x[GDN_TC - HALO:GDN_TC, :]
        w = w_ref[...]
        y = x * w[CONV_WIDTH - 1:CONV_WIDTH, :]
        for d in range(1, CONV_WIDTH):
            y = y + (xbuf_ref[idx, HALO - d:HALO - d + GDN_TC, :]
                     * w[CONV_WIDTH - 1 - d:CONV_WIDTH - d, :])
        return y * _sigmoid(y)

    q = conv_silu(q_ref, cwq_ref, 0)
    k = conv_silu(k_ref, cwk_ref, 1)
    v = conv_silu(v_ref, cwv_ref, 2)

    def l2n(x):
        return x * lax.rsqrt(jnp.sum(x * x, axis=-1, keepdims=True) + EPS)

    def head(x, hl):
        return x[:, hl * LANES:(hl + 1) * LANES]

    qn = [l2n(head(q, hl)) * (GDN_DK ** -0.5) for hl in range(GDN_HB)]
    kn = [l2n(head(k, hl)) for hl in range(GDN_HB)]

    sm_t = sm_ref[...].T
    rows_ref[...] = jnp.zeros_like(rows_ref)
    for c in range(GDN_NCH):
        pos = slice(c * CHUNK, (c + 1) * CHUNK)
        rows_ref[0, c * GDN_HB:(c + 1) * GDN_HB, 0:CHUNK] = sm_t[SM_GA:SM_GA + GDN_HB, pos]
        rows_ref[1, c * GDN_HB:(c + 1) * GDN_HB, 0:CHUNK] = sm_t[SM_GB:SM_GB + GDN_HB, pos]
    g = -jnp.exp(alog_ref[...]) * _softplus(rows_ref[0] + dtb_ref[...])
    lane = lax.broadcasted_iota(I32, (GDN_R, LANES), 1)
    gc = g
    sh = 1
    while sh < CHUNK:
        gc = gc + jnp.where(lane >= sh, pltpu.roll(gc, sh, 1), 0.0)
        sh *= 2
    beta = _sigmoid(rows_ref[1])
    zpad = jnp.zeros((LANES - GDN_R, LANES), F32)
    gc_t = jnp.concatenate([gc, zpad], axis=0).T
    beta_t = jnp.concatenate([beta, zpad], axis=0).T

    ri = lax.broadcasted_iota(I32, (CHUNK, CHUNK), 0)
    ci = lax.broadcasted_iota(I32, (CHUNK, CHUNK), 1)
    incl = ri >= ci
    strict = ri > ci
    eye = jnp.where(ri == ci, 1.0, 0.0).astype(F32)
    ng = ng_ref[...]
    rs = range(GDN_R)

    qc, kc, vc, g_col, b_col, decay, eg = [], [], [], [], [], [], []
    for r in rs:
        c, hl = divmod(r, GDN_HB)
        rows = slice(c * CHUNK, (c + 1) * CHUNK)
        qc.append(qn[hl][rows])
        kc.append(kn[hl][rows])
        vc.append(head(v, hl)[rows])
        g_col.append(jnp.broadcast_to(gc_t[0:CHUNK, r:r + 1], (CHUNK, LANES)))
        b_col.append(jnp.broadcast_to(beta_t[0:CHUNK, r:r + 1], (CHUNK, LANES)))
        diff = g_col[r][:, 0:CHUNK] - gc[r:r + 1, 0:CHUNK]
        decay.append(jnp.where(incl, jnp.exp(jnp.where(incl, diff, 0.0)), 0.0))
        eg.append(jnp.exp(g_col[r]))

    qk = [_dot_nt(jnp.concatenate([qc[r], kc[r]], axis=0).astype(BF16), kc[r].astype(BF16))
          for r in rs]

    xk = [-jnp.where(strict, b_col[r][:, 0:CHUNK] * qk[r][CHUNK:] * decay[r], 0.0) for r in rs]
    tm = [eye + xk[r] for r in rs]
    xk = [_dot(xk[r].astype(BF16), xk[r].astype(BF16)) for r in rs]
    for _ in range(4):
        st = [_dot(jnp.concatenate([tm[r], xk[r]], axis=0).astype(BF16), xk[r].astype(BF16))
              for r in rs]
        tm = [tm[r] + st[r][0:CHUNK] for r in rs]
        xk = [st[r][CHUNK:] for r in rs]
    tm = [tm[r] + _dot(tm[r].astype(BF16), xk[r].astype(BF16)) for r in rs]

    wu = [_dot(tm[r].astype(BF16),
               jnp.concatenate([kc[r] * (b_col[r] * eg[r]), vc[r] * b_col[r]], axis=1).astype(BF16)
               ).astype(BF16) for r in rs]
    awu = [_dot((qk[r][0:CHUNK] * decay[r]).astype(BF16), wu[r]) for r in rs]
    g_last = [g_col[r][CHUNK - 1:CHUNK, :] for r in rs]
    mn = [_dot_tn((kc[r] * jnp.exp(g_last[r] - g_col[r])).astype(BF16), wu[r]) for r in rs]
    lhs = [jnp.concatenate([qc[r] * eg[r] - awu[r][:, 0:LANES], mn[r][:, 0:LANES]],
                           axis=0).astype(BF16) for r in rs]
    el = [jnp.exp(g_last[r]) for r in rs]

    s = [s_ref[hl] for hl in range(GDN_HB)]
    outs = {}
    for c in range(GDN_NCH):
        for hl in range(GDN_HB):
            r = c * GDN_HB + hl
            st = _dot(lhs[r], s[hl].astype(BF16))
            outs[r] = st[0:CHUNK] + awu[r][:, LANES:]
            s[hl] = s[hl] * el[r] - st[CHUNK:] + mn[r][:, LANES:]
    for hl in range(GDN_HB):
        s_ref[hl] = s[hl]
    for r in rs:
        c, hl = divmod(r, GDN_HB)
        o = outs[r]
        o = o * lax.rsqrt(jnp.mean(o * o, axis=-1, keepdims=True) + EPS) * ng
        z = z_ref[c * CHUNK:(c + 1) * CHUNK, hl * LANES:(hl + 1) * LANES].astype(F32)
        o_ref[c * CHUNK:(c + 1) * CHUNK, hl * LANES:(hl + 1) * LANES] = (
            o * (z * _sigmoid(z))).astype(BF16)


def _gdn(main, small, conv_w, alog_rows, dtb_rows, norm_g, batch, seq):
    nt = seq // GDN_TC
    hw = GDN_HB * LANES

    def tok(col0):
        return pl.BlockSpec((GDN_TC, hw), lambda b, n, c=col0 // hw: (b * nt + n, c))

    def cw(col0):
        return pl.BlockSpec((CONV_WIDTH, hw), lambda b, n, c=col0 // hw: (0, c))

    par_spec = pl.BlockSpec((GDN_R, LANES), lambda b, n: (0, 0))
    return pl.pallas_call(
        _gdn_kernel,
        grid=(batch, nt),
        in_specs=[tok(OFF_GQ), tok(OFF_GK), tok(OFF_GV), tok(OFF_GZ),
                  cw(0), cw(GDN_HEADS * GDN_DK), cw(2 * GDN_HEADS * GDN_DK),
                  pl.BlockSpec((GDN_TC, LANES), lambda b, n: (b * nt + n, 0)),
                  par_spec, par_spec,
                  pl.BlockSpec((1, LANES), lambda b, n: (0, 0))],
        out_specs=pl.BlockSpec((GDN_TC, hw), lambda b, n: (b * nt + n, 0)),
        out_shape=jax.ShapeDtypeStruct((batch * seq, GDN_HEADS * GDN_DV), BF16),
        scratch_shapes=[pltpu.VMEM((GDN_HB, GDN_DK, GDN_DV), F32),
                        pltpu.VMEM((3, HALO + GDN_TC, hw), F32),
                        pltpu.VMEM((3, HALO, hw), F32),
                        pltpu.VMEM((2, GDN_R, LANES), F32)],
        compiler_params=pltpu.CompilerParams(
            dimension_semantics=("arbitrary", "arbitrary"), vmem_limit_bytes=VMEM_LIMIT),
        name="gdn",
    )(main, main, main, main, conv_w, conv_w, conv_w, small, alog_rows, dtb_rows, norm_g)


DSA_TQ = 256
DSA_TK = 256
REP = DSA_HEADS // DSA_KV_HEADS


def _dsa_kernel(q_ref, qi_ref, sm_ref, k_ref, v_ref, ki_ref, z_ref, o_ref,
                keys_ref, qih_ref, acc_ref, m_ref, l_ref, *, k_sel):
    i = pl.program_id(1)
    nkb = i + 1

    for hh in range(IDX_HEADS):
        qih_ref[hh] = qi_ref[:, hh * IDX_DIM:(hh + 1) * IDX_DIM]
    wt = sm_ref[...].T[SM_IW:SM_IW + IDX_HEADS, :] * (IDX_HEADS ** -0.5 * IDX_DIM ** -0.5)
    qpos = i * DSA_TQ + lax.broadcasted_iota(I32, (DSA_TK, DSA_TQ), 1)
    krow = lax.broadcasted_iota(I32, (DSA_TK, DSA_TQ), 0)

    def p1(kb, carry):
        off = pl.multiple_of(kb * DSA_TK, DSA_TK)
        kblk = ki_ref[pl.ds(off, DSA_TK), :]
        acc = jnp.zeros((DSA_TK, DSA_TQ), F32)
        for hh in range(IDX_HEADS):
            p = _dot_nt(kblk, qih_ref[hh])
            acc = acc + wt[hh:hh + 1, :] * jnp.maximum(p, 0.0)
        keys_ref[pl.ds(off, DSA_TK), :] = jnp.where(kb * DSA_TK + krow <= qpos, acc, -jnp.inf)
        return carry

    lax.fori_loop(0, nkb, p1, 0)

    def key_to_f32(key):
        return pltpu.bitcast(key ^ ((key >> 31) & 0x7FFFFFFF), F32)

    def p2(p, tau):
        cand = tau ^ lax.shift_left(jnp.int32(1), 31 - p)
        cand_f = key_to_f32(cand)

        def cnt_body(kb, cnt):
            off = pl.multiple_of(kb * DSA_TK, DSA_TK)
            hit = jnp.where(keys_ref[pl.ds(off, DSA_TK), :] >= cand_f, 1, 0)
            parts = [hit[r:r + SUBLANES, :] for r in range(0, DSA_TK, SUBLANES)]
            while len(parts) > 1:
                parts = [a + b for a, b in zip(parts[0::2], parts[1::2])]
            return cnt + parts[0]

        cnt = lax.fori_loop(0, nkb, cnt_body, jnp.zeros((SUBLANES, DSA_TQ), I32))
        cnt = jnp.sum(cnt, axis=0, keepdims=True)
        return jnp.where(cnt >= k_sel, cand, tau)

    tau = lax.fori_loop(0, 32, p2, jnp.full((1, DSA_TQ), INT_MIN, I32))
    tau = key_to_f32(jnp.maximum(tau, KEY_F32_LOWEST))

    m_ref[...] = jnp.full_like(m_ref, NEG_BIG)
    l_ref[...] = jnp.zeros_like(l_ref)
    acc_ref[...] = jnp.zeros_like(acc_ref)

    def p3(kb, carry):
        off = pl.multiple_of(kb * DSA_TK, DSA_TK)
        bias = jnp.where(keys_ref[pl.ds(off, DSA_TK), :] >= tau, 0.0, NEG_BIG)
        heads = range(DSA_HEADS)
        kblk = [k_ref[pl.ds(off, DSA_TK), g * DSA_HEAD_DIM:(g + 1) * DSA_HEAD_DIM]
                for g in range(DSA_KV_HEADS)]
        vblk = [v_ref[pl.ds(off, DSA_TK), g * DSA_HEAD_DIM:(g + 1) * DSA_HEAD_DIM]
                for g in range(DSA_KV_HEADS)]
        s = [_dot_nt(kblk[hh // REP], q_ref[:, hh * DSA_HEAD_DIM:(hh + 1) * DSA_HEAD_DIM]) + bias
             for hh in heads]
        m_old = [m_ref[hh] for hh in heads]
        m_new = [jnp.maximum(m_old[hh], jnp.max(s[hh], axis=0, keepdims=True)) for hh in heads]
        alpha = [jnp.exp2(m_old[hh] - m_new[hh]) for hh in heads]
        p = [jnp.exp2(s[hh] - m_new[hh][0:1, :]) for hh in heads]
        for hh in heads:
            l_ref[hh] = alpha[hh] * l_ref[hh] + jnp.sum(p[hh], axis=0, keepdims=True)
            m_ref[hh] = m_new[hh]
        pv = [_dot_tn(vblk[hh // REP], p[hh].astype(BF16)) for hh in heads]
        for hh in heads:
            acc_ref[hh] = alpha[hh][0:1, :] * acc_ref[hh] + pv[hh]
        return carry

    lax.fori_loop(0, nkb, p3, 0)

    for hh in range(DSA_HEADS):
        o_t = acc_ref[hh] / l_ref[hh][0:1, :]
        z = z_ref[:, hh * DSA_HEAD_DIM:(hh + 1) * DSA_HEAD_DIM].astype(F32)
        o_ref[:, hh * DSA_HEAD_DIM:(hh + 1) * DSA_HEAD_DIM] = (
            o_t.T * (z * _sigmoid(z))).astype(BF16)


def _dsa(main, small, k_idx, batch, seq):
    nq = seq // DSA_TQ
    k_sel = min(TOPK_MAX, seq // 4)
    width = DSA_HEADS * DSA_HEAD_DIM
    kvw = DSA_KV_HEADS * DSA_HEAD_DIM

    def tokw(col0):
        return pl.BlockSpec((DSA_TQ, width), lambda b, i, c=col0 // width: (b * nq + i, c))

    return pl.pallas_call(
        functools.partial(_dsa_kernel, k_sel=k_sel),
        grid=(batch, nq),
        in_specs=[
            tokw(OFF_AQ), tokw(OFF_IQ),
            pl.BlockSpec((DSA_TQ, LANES), lambda b, i: (b * nq + i, 0)),
            pl.BlockSpec((seq, kvw), lambda b, i: (b, OFF_AK // kvw)),
            pl.BlockSpec((seq, kvw), lambda b, i: (b, OFF_AV // kvw)),
            pl.BlockSpec((seq, IDX_DIM), lambda b, i: (b, 0)),
            tokw(OFF_AZ),
        ],
        out_specs=pl.BlockSpec((DSA_TQ, width), lambda b, i: (b * nq + i, 0)),
        out_shape=jax.ShapeDtypeStruct((batch * seq, width), BF16),
        scratch_shapes=[
            pltpu.VMEM((seq, DSA_TQ), F32),
            pltpu.VMEM((IDX_HEADS, DSA_TQ, IDX_DIM), BF16),
            pltpu.VMEM((DSA_HEADS, DSA_HEAD_DIM, DSA_TQ), F32),
            pltpu.VMEM((DSA_HEADS, SUBLANES, DSA_TQ), F32),
            pltpu.VMEM((DSA_HEADS, SUBLANES, DSA_TQ), F32),
        ],
        compiler_params=pltpu.CompilerParams(
            dimension_semantics=("arbitrary", "arbitrary"), vmem_limit_bytes=VMEM_LIMIT),
        name="dsa",
    )(main, main, small, main, main, k_idx, main)


OUT_TM = 512


def _out_kernel(oa_ref, ob_ref, x_ref, w_ref, g_ref, out_ref):
    wa = oa_ref.shape[1]
    mixed = _dot(oa_ref[...], w_ref[0:wa, :]) + _dot(ob_ref[...], w_ref[wa:, :])
    y = x_ref[...] + mixed
    ms = jnp.mean(y * y, axis=-1, keepdims=True)
    out_ref[...] = y * lax.rsqrt(ms + EPS) * g_ref[...]


def _output_projection(o_a, o_b, xf, w_out, g):
    n = xf.shape[0]
    wa, wb = o_a.shape[1], o_b.shape[1]
    return pl.pallas_call(
        _out_kernel,
        grid=(n // OUT_TM,),
        in_specs=[
            pl.BlockSpec((OUT_TM, wa), lambda i: (i, 0)),
            pl.BlockSpec((OUT_TM, wb), lambda i: (i, 0)),
            pl.BlockSpec((OUT_TM, D_MODEL), lambda i: (i, 0)),
            pl.BlockSpec((wa + wb, D_MODEL), lambda i: (0, 0)),
            pl.BlockSpec((1, D_MODEL), lambda i: (0, 0)),
        ],
        out_specs=pl.BlockSpec((OUT_TM, D_MODEL), lambda i: (i, 0)),
        out_shape=jax.ShapeDtypeStruct((n, D_MODEL), F32),
        compiler_params=pltpu.CompilerParams(
            dimension_semantics=("arbitrary",), vmem_limit_bytes=VMEM_LIMIT),
        name="out_proj",
    )(o_a, o_b, xf, w_out, g)


def _rope_tables(positions, head_dim):
    rot = head_dim // ROPE_FRACTION
    half = rot // 2
    inv_freq = ROPE_THETA ** (-(jnp.arange(half, dtype=F32) * 2.0 / rot))
    ang = positions.astype(F32).reshape(-1, 1) * inv_freq
    cos, sin = jnp.cos(ang), jnp.sin(ang)
    n = ang.shape[0]
    rest = head_dim - rot
    c = jnp.concatenate([cos, cos, jnp.ones((n, rest), F32)], axis=-1)
    s = jnp.concatenate([-sin, sin, jnp.zeros((n, rest), F32)], axis=-1)
    return jnp.tile(jnp.stack([c, s]), (1, 1, LANES // head_dim))


def _split_w_in(w):
    w_a = w[:, 0:WA_COLS].astype(BF16)
    w_b = w[:, WB_SRC0:WB_SRC0 + WB_COLS].astype(BF16)
    gab = w[:, WA_COLS:WB_SRC0]
    ikw = w[:, WB_SRC0 + WB_COLS:]
    pad = jnp.zeros((w.shape[0], LANES - gab.shape[1] - ikw.shape[1]), w.dtype)
    w_small = jnp.concatenate([ikw, gab, pad], axis=1).astype(BF16)
    return w_a, w_b, w_small


def kernel(x, positions, attn_norm_g, w_in, gdn_conv_w, gdn_a_log, gdn_dt_bias, gdn_norm_g,
           w_out, final_norm_g):
    batch, seq, d = x.shape
    assert d == D_MODEL and w_in.shape[0] == 1, "single-layer trunk with D_MODEL=2048 only"
    assert seq % max(GDN_TC, DSA_TQ) == 0 and (batch * seq) % max(PROJ_TM, OUT_TM) == 0
    xf = x.reshape(batch * seq, d)

    w_a, w_b, w_small = _split_w_in(w_in[0])
    main, small, k_idx = _input_projection(xf, attn_norm_g[0].reshape(1, d), w_a, w_b, w_small,
                                           _rope_tables(positions, DSA_HEAD_DIM),
                                           _rope_tables(positions, IDX_DIM))

    def head_rows(p):
        return jnp.broadcast_to(jnp.tile(p, GDN_NCH)[:, None], (GDN_R, LANES))

    o_a = _gdn(main, small, gdn_conv_w[0], head_rows(gdn_a_log[0]), head_rows(gdn_dt_bias[0]),
               gdn_norm_g[0].reshape(1, GDN_DV), batch, seq)
    o_b = _dsa(main, small, k_idx, batch, seq)

    out = _output_projection(o_a, o_b, xf, w_out[0].astype(BF16), final_norm_g.reshape(1, d))
    return out.reshape(batch, seq, d)
```

```python
import functools
import math

import jax
import jax.numpy as jnp
from jax import lax
from jax.experimental import pallas as pl
from jax.experimental.pallas import tpu as pltpu

F32 = jnp.float32
BF16 = jnp.bfloat16
I32 = jnp.int32

D_MODEL = 2048
GDN_HEADS = 8
GDN_DK = 128
GDN_DV = 128
CONV_WIDTH = 4
CHUNK = 64
DSA_HEADS = 8
DSA_KV_HEADS = 2
DSA_HEAD_DIM = 128
IDX_HEADS = 16
IDX_DIM = 64
TOPK_MAX = 256
ROPE_THETA = 500000.0
ROPE_FRACTION = 4
EPS = 1e-6

LANES = 128
SUBLANES = 8
PACKED_ROWS = 16
HALO = SUBLANES
VMEM_LIMIT = 56 * 1024 * 1024

OFF_GQ, OFF_GK, OFF_GV, OFF_GZ = 0, 1024, 2048, 3072
OFF_AQ, OFF_AZ, OFF_IQ, OFF_AK, OFF_AV = 4096, 5120, 6144, 7168, 7424
MAIN_COLS = 7680
WA_COLS = 4096
WB_SRC0 = 4112
WB_COLS = 3584
WB_MAP = ((0, 1024, OFF_AQ), (1024, 256, OFF_AK), (1280, 256, OFF_AV),
          (1536, 1024, OFF_AZ), (2560, 1024, OFF_IQ))
SM_IK, SM_IW, SM_GA, SM_GB = 0, 64, 80, 88

INT_MIN = -2147483648
KEY_F32_LOWEST = INT_MIN + 0x00800000
NEG_BIG = -1e30


def _sigmoid(x):
    return 1.0 / (1.0 + jnp.exp(-x))


def _softplus(x):
    return jnp.maximum(x, 0.0) + jnp.log(1.0 + jnp.exp(-jnp.abs(x)))


def _dot(a, b):
    return jnp.dot(a, b, preferred_element_type=F32)


def _dot_nt(a, b):
    return lax.dot_general(a, b, (((1,), (1,)), ((), ())), preferred_element_type=F32)


def _dot_tn(a, b):
    return lax.dot_general(a, b, (((0,), (0,)), ((), ())), preferred_element_type=F32)


PROJ_TM = 256
PROJ_SEG = 512
HALF128 = DSA_HEAD_DIM // ROPE_FRACTION // 2
HALF64 = IDX_DIM // ROPE_FRACTION // 2
Q_SCALE = DSA_HEAD_DIM ** -0.5 * math.log2(math.e)


def _rope_slab(xs, tab_ref, half, head_dim):
    n = xs.shape[-1]
    lane = lax.broadcasted_iota(I32, xs.shape, 1) % head_dim
    partner = jnp.where(lane < half, pltpu.roll(xs, n - half, 1), pltpu.roll(xs, half, 1))
    return xs * tab_ref[0] + partner * tab_ref[1]


def _col_kind(col):
    if OFF_GQ <= col < OFF_GK:
        return "gq"
    if OFF_GK <= col < OFF_GV:
        return "gk"
    if OFF_GV <= col < OFF_GZ:
        return "gv"
    if OFF_GZ <= col < OFF_AQ or OFF_AZ <= col < OFF_IQ:
        return "gate"
    if OFF_AQ <= col < OFF_AQ + DSA_HEADS * DSA_HEAD_DIM:
        return "q"
    if OFF_AK <= col < OFF_AK + DSA_KV_HEADS * DSA_HEAD_DIM:
        return "k"
    if OFF_IQ <= col < OFF_IQ + IDX_HEADS * IDX_DIM:
        return "i"
    return "plain"


def _silu(x):
    return x * _sigmoid(x)


def _l2n(x):
    return x * lax.rsqrt(jnp.sum(x * x, axis=-1, keepdims=True) + EPS)


def _wb_dest(col):
    for src0, width, dst0 in WB_MAP:
        if src0 <= col < src0 + width:
            return dst0 + col - src0
    raise ValueError(col)


def _proj_kernel(x_ref, g_ref, wa_ref, wb_ref, ws_ref, cw_ref, t128_ref, t64_ref,
                 main_ref, small_ref, kidx_ref, carry_ref, *, tiles_per_seq):
    i = pl.program_id(0)

    @pl.when(i % tiles_per_seq == 0)
    def _():
        carry_ref[...] = jnp.zeros_like(carry_ref)

    x = x_ref[...]
    ms = jnp.mean(x * x, axis=-1, keepdims=True)
    h = (x * lax.rsqrt(ms + EPS) * g_ref[...]).astype(BF16)
    s = _dot(h, ws_ref[...])
    lane = lax.broadcasted_iota(I32, s.shape, 1)
    s = jnp.where(lane < IDX_DIM, _rope_slab(s, t64_ref, HALF64, IDX_DIM), s)
    small_ref[...] = s
    kidx_ref[...] = s[:, SM_IK:SM_IK + IDX_DIM].astype(BF16)

    groups = PROJ_TM // SUBLANES
    sub = lax.broadcasted_iota(I32, (groups, SUBLANES, PROJ_SEG), 1)

    def conv_silu(acc, c0):
        seg = c0 // PROJ_SEG
        x3 = jnp.concatenate([carry_ref[seg], acc], axis=0).reshape(groups + 1, SUBLANES, PROJ_SEG)
        carry_ref[seg] = acc[PROJ_TM - HALO:PROJ_TM, :]
        w = cw_ref[:, c0:c0 + PROJ_SEG]
        y = acc * w[CONV_WIDTH - 1:CONV_WIDTH, :]
        for d in range(1, CONV_WIDTH):
            rot = pltpu.roll(x3, d, 1)
            shifted = jnp.where(sub < d, rot[0:groups], rot[1:groups + 1])
            y = y + shifted.reshape(PROJ_TM, PROJ_SEG) * w[CONV_WIDTH - 1 - d:CONV_WIDTH - d, :]
        return _silu(y)

    def epilogue(acc, c0, dest):
        if _col_kind(dest(c0)) in ("gq", "gk", "gv"):
            acc = conv_silu(acc, c0)
        for c in range(0, PROJ_SEG, LANES):
            val = acc[:, c:c + LANES]
            d0 = dest(c0 + c)
            kind = _col_kind(d0)
            if kind == "gq":
                val = _l2n(val) * (GDN_DK ** -0.5)
            elif kind == "gk":
                val = _l2n(val)
            elif kind == "gate":
                val = _silu(val)
            elif kind == "q":
                val = _rope_slab(val, t128_ref, HALF128, DSA_HEAD_DIM) * Q_SCALE
            elif kind == "k":
                val = _rope_slab(val, t128_ref, HALF128, DSA_HEAD_DIM)
            elif kind == "i":
                val = _rope_slab(val, t64_ref, HALF64, IDX_DIM)
            main_ref[:, d0:d0 + LANES] = val.astype(BF16)

    segments = ([(wa_ref, c0, lambda col: col) for c0 in range(0, WA_COLS, PROJ_SEG)]
                + [(wb_ref, c0, _wb_dest) for c0 in range(0, WB_COLS, PROJ_SEG)])
    pending = None
    for w_ref, c0, dest in segments:
        acc = _dot(h, w_ref[:, c0:c0 + PROJ_SEG])
        if pending is not None:
            epilogue(*pending)
        pending = (acc, c0, dest)
    epilogue(*pending)


def _input_projection(xf, g, w_a, w_b, w_small, conv_w, tab128, tab64, seq):
    n = xf.shape[0]
    tab_spec = pl.BlockSpec((2, PROJ_TM, LANES), lambda i: (0, i, 0))
    resident = pl.Buffered(1)
    conv_cols = conv_w.shape[1]
    assert conv_cols == OFF_GZ and conv_cols % PROJ_SEG == 0
    return pl.pallas_call(
        functools.partial(_proj_kernel, tiles_per_seq=seq // PROJ_TM),
        grid=(n // PROJ_TM,),
        in_specs=[
            pl.BlockSpec((PROJ_TM, D_MODEL), lambda i: (i, 0)),
            pl.BlockSpec((1, D_MODEL), lambda i: (0, 0)),
            pl.BlockSpec((D_MODEL, WA_COLS), lambda i: (0, 0), pipeline_mode=resident),
            pl.BlockSpec((D_MODEL, WB_COLS), lambda i: (0, 0), pipeline_mode=resident),
            pl.BlockSpec((D_MODEL, LANES), lambda i: (0, 0), pipeline_mode=resident),
            pl.BlockSpec((CONV_WIDTH, conv_cols), lambda i: (0, 0), pipeline_mode=resident),
            tab_spec, tab_spec,
        ],
        out_specs=[
            pl.BlockSpec((PROJ_TM, MAIN_COLS), lambda i: (i, 0)),
            pl.BlockSpec((PROJ_TM, LANES), lambda i: (i, 0)),
            pl.BlockSpec((PROJ_TM, IDX_DIM), lambda i: (i, 0)),
        ],
        out_shape=[
            jax.ShapeDtypeStruct((n, MAIN_COLS), BF16),
            jax.ShapeDtypeStruct((n, LANES), F32),
            jax.ShapeDtypeStruct((n, IDX_DIM), BF16),
        ],
        scratch_shapes=[
            pltpu.VMEM((conv_cols // PROJ_SEG, HALO, PROJ_SEG), F32),
        ],
        compiler_params=pltpu.CompilerParams(
            dimension_semantics=("arbitrary",), vmem_limit_bytes=VMEM_LIMIT),
        name="in_proj",
    )(xf, g, w_a, w_b, w_small, conv_w, tab128, tab64)


GDN_TC = 256
GDN_NCH = GDN_TC // CHUNK
GDN_HB = GDN_HEADS
GDN_R = GDN_HB * GDN_NCH


def _gdn_kernel(q_ref, k_ref, v_ref, z_ref, sm_ref, alog_ref, dtb_ref, ng_ref,
                o_ref, s_ref, rows_ref):
    n = pl.program_id(1)

    @pl.when(n == 0)
    def _():
        s_ref[...] = jnp.zeros_like(s_ref)

    v = v_ref[...].astype(F32)

    def head(x, hl):
        return x[:, hl * LANES:(hl + 1) * LANES]

    qn = [q_ref[:, hl * LANES:(hl + 1) * LANES].astype(F32) for hl in range(GDN_HB)]
    kn = [k_ref[:, hl * LANES:(hl + 1) * LANES].astype(F32) for hl in range(GDN_HB)]

    sm_t = sm_ref[...].T
    rows_ref[...] = jnp.zeros_like(rows_ref)
    for c in range(GDN_NCH):
        pos = slice(c * CHUNK, (c + 1) * CHUNK)
        rows_ref[0, c * GDN_HB:(c + 1) * GDN_HB, 0:CHUNK] = sm_t[SM_GA:SM_GA + GDN_HB, pos]
        rows_ref[1, c * GDN_HB:(c + 1) * GDN_HB, 0:CHUNK] = sm_t[SM_GB:SM_GB + GDN_HB, pos]
    g = -jnp.exp(alog_ref[...]) * _softplus(rows_ref[0] + dtb_ref[...])
    lane = lax.broadcasted_iota(I32, (GDN_R, LANES), 1)
    gc = g
    sh = 1
    while sh < CHUNK:
        gc = gc + jnp.where(lane >= sh, pltpu.roll(gc, sh, 1), 0.0)
        sh *= 2
    beta = _sigmoid(rows_ref[1])
    zpad = jnp.zeros((LANES - GDN_R, LANES), F32)
    gc_t = jnp.concatenate([gc, zpad], axis=0).T
    beta_t = jnp.concatenate([beta, zpad], axis=0).T

    ri = lax.broadcasted_iota(I32, (CHUNK, CHUNK), 0)
    ci = lax.broadcasted_iota(I32, (CHUNK, CHUNK), 1)
    incl = ri >= ci
    strict = ri > ci
    eye = jnp.where(ri == ci, 1.0, 0.0).astype(F32)
    ng = ng_ref[...]
    rs = range(GDN_R)

    qc, kc, vc, g_col, b_col, decay, eg = [], [], [], [], [], [], []
    for r in rs:
        c, hl = divmod(r, GDN_HB)
        rows = slice(c * CHUNK, (c + 1) * CHUNK)
        qc.append(qn[hl][rows])
        kc.append(kn[hl][rows])
        vc.append(head(v, hl)[rows])
        g_col.append(jnp.broadcast_to(gc_t[0:CHUNK, r:r + 1], (CHUNK, LANES)))
        b_col.append(jnp.broadcast_to(beta_t[0:CHUNK, r:r + 1], (CHUNK, LANES)))
        diff = g_col[r][:, 0:CHUNK] - gc[r:r + 1, 0:CHUNK]
        decay.append(jnp.where(incl, jnp.exp(jnp.where(incl, diff, 0.0)), 0.0))
        eg.append(jnp.exp(g_col[r]))

    qk = [_dot_nt(jnp.concatenate([qc[r], kc[r]], axis=0).astype(BF16), kc[r].astype(BF16))
          for r in rs]

    xk = [-jnp.where(strict, b_col[r][:, 0:CHUNK] * qk[r][CHUNK:] * decay[r], 0.0) for r in rs]
    tm = [eye + xk[r] for r in rs]
    xk = [_dot(xk[r].astype(BF16), xk[r].astype(BF16)) for r in rs]
    for _ in range(4):
        st = [_dot(jnp.concatenate([tm[r], xk[r]], axis=0).astype(BF16), xk[r].astype(BF16))
              for r in rs]
        tm = [tm[r] + st[r][0:CHUNK] for r in rs]
        xk = [st[r][CHUNK:] for r in rs]
    tm = [tm[r] + _dot(tm[r].astype(BF16), xk[r].astype(BF16)) for r in rs]

    wu = [_dot(tm[r].astype(BF16),
               jnp.concatenate([kc[r] * (b_col[r] * eg[r]), vc[r] * b_col[r]], axis=1).astype(BF16)
               ).astype(BF16) for r in rs]
    awu = [_dot((qk[r][0:CHUNK] * decay[r]).astype(BF16), wu[r]) for r in rs]
    g_last = [g_col[r][CHUNK - 1:CHUNK, :] for r in rs]
    mn = [_dot_tn((kc[r] * jnp.exp(g_last[r] - g_col[r])).astype(BF16), wu[r]) for r in rs]
    lhs = [jnp.concatenate([qc[r] * eg[r] - awu[r][:, 0:LANES], mn[r][:, 0:LANES]],
                           axis=0).astype(BF16) for r in rs]
    el = [jnp.exp(g_last[r]) for r in rs]

    s = [s_ref[hl] for hl in range(GDN_HB)]
    outs = {}
    for c in range(GDN_NCH):
        for hl in range(GDN_HB):
            r = c * GDN_HB + hl
            st = _dot(lhs[r], s[hl].astype(BF16))
            outs[r] = st[0:CHUNK] + awu[r][:, LANES:]
            s[hl] = s[hl] * el[r] - st[CHUNK:] + mn[r][:, LANES:]
    for hl in range(GDN_HB):
        s_ref[hl] = s[hl]
    for r in rs:
        c, hl = divmod(r, GDN_HB)
        o = outs[r]
        o = o * lax.rsqrt(jnp.mean(o * o, axis=-1, keepdims=True) + EPS) * ng
        gate = z_ref[c * CHUNK:(c + 1) * CHUNK, hl * LANES:(hl + 1) * LANES].astype(F32)
        o_ref[c * CHUNK:(c + 1) * CHUNK, hl * LANES:(hl + 1) * LANES] = (o * gate).astype(BF16)


def _gdn(main, small, alog_rows, dtb_rows, norm_g, batch, seq):
    nt = seq // GDN_TC
    hw = GDN_HB * LANES

    def tok(col0):
        return pl.BlockSpec((GDN_TC, hw), lambda b, n, c=col0 // hw: (b * nt + n, c))

    par_spec = pl.BlockSpec((GDN_R, LANES), lambda b, n: (0, 0))
    return pl.pallas_call(
        _gdn_kernel,
        grid=(batch, nt),
        in_specs=[tok(OFF_GQ), tok(OFF_GK), tok(OFF_GV), tok(OFF_GZ),
                  pl.BlockSpec((GDN_TC, LANES), lambda b, n: (b * nt + n, 0)),
                  par_spec, par_spec,
                  pl.BlockSpec((1, LANES), lambda b, n: (0, 0))],
        out_specs=pl.BlockSpec((GDN_TC, hw), lambda b, n: (b * nt + n, 0)),
        out_shape=jax.ShapeDtypeStruct((batch * seq, GDN_HEADS * GDN_DV), BF16),
        scratch_shapes=[pltpu.VMEM((GDN_HB, GDN_DK, GDN_DV), F32),
                        pltpu.VMEM((2, GDN_R, LANES), F32)],
        compiler_params=pltpu.CompilerParams(
            dimension_semantics=("arbitrary", "arbitrary"), vmem_limit_bytes=VMEM_LIMIT),
        name="gdn",
    )(main, main, main, main, small, alog_rows, dtb_rows, norm_g)


DSA_TQ = 256
DSA_TK = 256
REP = DSA_HEADS // DSA_KV_HEADS


def _dsa_kernel(q_ref, qi_ref, sm_ref, k_ref, v_ref, ki_ref, z_ref, o_ref,
                keys_ref, keys16_ref, qih_ref, acc_ref, m_ref, l_ref, *, k_sel):
    i = pl.program_id(1)
    nkb = i + 1

    for hh in range(IDX_HEADS):
        qih_ref[hh] = qi_ref[:, hh * IDX_DIM:(hh + 1) * IDX_DIM]
    wt = sm_ref[...].T[SM_IW:SM_IW + IDX_HEADS, :] * (IDX_HEADS ** -0.5 * IDX_DIM ** -0.5)
    qpos = i * DSA_TQ + lax.broadcasted_iota(I32, (DSA_TK, DSA_TQ), 1)
    krow = lax.broadcasted_iota(I32, (DSA_TK, DSA_TQ), 0)

    def p1(kb, carry):
        off = pl.multiple_of(kb * DSA_TK, DSA_TK)
        kblk = ki_ref[pl.ds(off, DSA_TK), :]
        acc = jnp.zeros((DSA_TK, DSA_TQ), F32)
        for hh in range(IDX_HEADS):
            p = _dot_nt(kblk, qih_ref[hh])
            acc = acc + wt[hh:hh + 1, :] * jnp.maximum(p, 0.0)
        sc = jnp.where(kb * DSA_TK + krow <= qpos, acc, -jnp.inf)
        keys_ref[pl.ds(off, DSA_TK), :] = sc
        keys16_ref[pl.ds(off, DSA_TK), :] = sc.astype(BF16)
        return carry

    lax.fori_loop(0, nkb, p1, 0)

    def tree_sum(x, rows):
        parts = [x[r:r + rows, :] for r in range(0, x.shape[0], rows)]
        while len(parts) > 1:
            parts = [a + b for a, b in zip(parts[0::2], parts[1::2])]
        return parts[0]

    def key_to_bits(key, width):
        return key ^ ((key >> (width - 1)) & ((1 << (width - 1)) - 1))

    def p2a(p, u16):
        cand = u16 | lax.shift_left(jnp.int32(1), 15 - p)
        bits = lax.shift_left(key_to_bits(cand - 32768, 16), 16)
        c16 = pltpu.bitcast(bits, F32).astype(BF16)

        def cnt_body(kb, cnt):
            off = pl.multiple_of(kb * DSA_TK, DSA_TK)
            hit = jnp.where(keys16_ref[pl.ds(off, DSA_TK), :] >= c16,
                            jnp.ones((), BF16), jnp.zeros((), BF16))
            return cnt + tree_sum(hit, PACKED_ROWS).astype(F32)

        cnt = lax.fori_loop(0, nkb, cnt_body, jnp.zeros((PACKED_ROWS, DSA_TQ), F32))
        cnt = jnp.sum(cnt, axis=0, keepdims=True)
        return jnp.where(cnt >= k_sel, cand, u16)

    u16 = lax.fori_loop(0, 16, p2a, jnp.zeros((1, DSA_TQ), I32))
    coarse = lax.shift_left(key_to_bits(u16 - 32768, 16), 16)
    lo_key = key_to_bits(coarse, 32) - (1 << 15)

    def p2b(p, off_key):
        cand = off_key | lax.shift_left(jnp.int32(1), 16 - p)
        cand_f = pltpu.bitcast(key_to_bits(lo_key + cand, 32), F32)

        def cnt_body(kb, cnt):
            off = pl.multiple_of(kb * DSA_TK, DSA_TK)
            hit = jnp.where(keys_ref[pl.ds(off, DSA_TK), :] >= cand_f, 1, 0)
            return cnt + tree_sum(hit, SUBLANES)

        cnt = lax.fori_loop(0, nkb, cnt_body, jnp.zeros((SUBLANES, DSA_TQ), I32))
        cnt = jnp.sum(cnt, axis=0, keepdims=True)
        return jnp.where(cnt >= k_sel, cand, off_key)

    off_key = lax.fori_loop(0, 17, p2b, jnp.zeros((1, DSA_TQ), I32))
    tau = pltpu.bitcast(key_to_bits(jnp.maximum(lo_key + off_key, KEY_F32_LOWEST), 32), F32)

    m_ref[...] = jnp.full_like(m_ref, NEG_BIG)
    l_ref[...] = jnp.zeros_like(l_ref)
    acc_ref[...] = jnp.zeros_like(acc_ref)

    def p3(kb, carry):
        off = pl.multiple_of(kb * DSA_TK, DSA_TK)
        bias = jnp.where(keys_ref[pl.ds(off, DSA_TK), :] >= tau, 0.0, NEG_BIG)
        heads = range(DSA_HEADS)
        kblk = [k_ref[pl.ds(off, DSA_TK), g * DSA_HEAD_DIM:(g + 1) * DSA_HEAD_DIM]
                for g in range(DSA_KV_HEADS)]
        vblk = [v_ref[pl.ds(off, DSA_TK), g * DSA_HEAD_DIM:(g + 1) * DSA_HEAD_DIM]
                for g in range(DSA_KV_HEADS)]
        s = [_dot_nt(kblk[hh // REP], q_ref[:, hh * DSA_HEAD_DIM:(hh + 1) * DSA_HEAD_DIM]) + bias
             for hh in heads]
        m_old = [m_ref[hh] for hh in heads]
        m_new = [jnp.maximum(m_old[hh], jnp.max(s[hh], axis=0, keepdims=True)) for hh in heads]
        alpha = [jnp.exp2(m_old[hh] - m_new[hh]) for hh in heads]
        p = [jnp.exp2(s[hh] - m_new[hh][0:1, :]) for hh in heads]
        for hh in heads:
            l_ref[hh] = alpha[hh] * l_ref[hh] + jnp.sum(p[hh], axis=0, keepdims=True)
            m_ref[hh] = m_new[hh]
        pv = [_dot_tn(vblk[hh // REP], p[hh].astype(BF16)) for hh in heads]
        for hh in heads:
            acc_ref[hh] = alpha[hh][0:1, :] * acc_ref[hh] + pv[hh]
        return carry

    lax.fori_loop(0, nkb, p3, 0)

    for hh in range(DSA_HEADS):
        o_t = acc_ref[hh] / l_ref[hh][0:1, :]
        gate = z_ref[:, hh * DSA_HEAD_DIM:(hh + 1) * DSA_HEAD_DIM].astype(F32)
        o_ref[:, hh * DSA_HEAD_DIM:(hh + 1) * DSA_HEAD_DIM] = (o_t.T * gate).astype(BF16)


def _dsa(main, small, k_idx, batch, seq):
    nq = seq // DSA_TQ
    k_sel = min(TOPK_MAX, seq // 4)
    width = DSA_HEADS * DSA_HEAD_DIM
    kvw = DSA_KV_HEADS * DSA_HEAD_DIM

    def tokw(col0):
        return pl.BlockSpec((DSA_TQ, width), lambda b, i, c=col0 // width: (b * nq + i, c))

    return pl.pallas_call(
        functools.partial(_dsa_kernel, k_sel=k_sel),
        grid=(batch, nq),
        in_specs=[
            tokw(OFF_AQ), tokw(OFF_IQ),
            pl.BlockSpec((DSA_TQ, LANES), lambda b, i: (b * nq + i, 0)),
            pl.BlockSpec((seq, kvw), lambda b, i: (b, OFF_AK // kvw)),
            pl.BlockSpec((seq, kvw), lambda b, i: (b, OFF_AV // kvw)),
            pl.BlockSpec((seq, IDX_DIM), lambda b, i: (b, 0)),
            tokw(OFF_AZ),
        ],
        out_specs=pl.BlockSpec((DSA_TQ, width), lambda b, i: (b * nq + i, 0)),
        out_shape=jax.ShapeDtypeStruct((batch * seq, width), BF16),
        scratch_shapes=[
            pltpu.VMEM((seq, DSA_TQ), F32),
            pltpu.VMEM((seq, DSA_TQ), BF16),
            pltpu.VMEM((IDX_HEADS, DSA_TQ, IDX_DIM), BF16),
            pltpu.VMEM((DSA_HEADS, DSA_HEAD_DIM, DSA_TQ), F32),
            pltpu.VMEM((DSA_HEADS, SUBLANES, DSA_TQ), F32),
            pltpu.VMEM((DSA_HEADS, SUBLANES, DSA_TQ), F32),
        ],
        compiler_params=pltpu.CompilerParams(
            dimension_semantics=("arbitrary", "arbitrary"), vmem_limit_bytes=VMEM_LIMIT),
        name="dsa",
    )(main, main, small, main, main, k_idx, main)


OUT_TM = 512


def _out_kernel(oa_ref, ob_ref, x_ref, w_ref, g_ref, out_ref):
    wa = oa_ref.shape[1]
    mixed = _dot(oa_ref[...], w_ref[0:wa, :]) + _dot(ob_ref[...], w_ref[wa:, :])
    y = x_ref[...] + mixed
    ms = jnp.mean(y * y, axis=-1, keepdims=True)
    out_ref[...] = y * lax.rsqrt(ms + EPS) * g_ref[...]


def _output_projection(o_a, o_b, xf, w_out, g):
    n = xf.shape[0]
    wa, wb = o_a.shape[1], o_b.shape[1]
    return pl.pallas_call(
        _out_kernel,
        grid=(n // OUT_TM,),
        in_specs=[
            pl.BlockSpec((OUT_TM, wa), lambda i: (i, 0)),
            pl.BlockSpec((OUT_TM, wb), lambda i: (i, 0)),
            pl.BlockSpec((OUT_TM, D_MODEL), lambda i: (i, 0)),
            pl.BlockSpec((wa + wb, D_MODEL), lambda i: (0, 0)),
            pl.BlockSpec((1, D_MODEL), lambda i: (0, 0)),
        ],
        out_specs=pl.BlockSpec((OUT_TM, D_MODEL), lambda i: (i, 0)),
        out_shape=jax.ShapeDtypeStruct((n, D_MODEL), F32),
        compiler_params=pltpu.CompilerParams(
            dimension_semantics=("arbitrary",), vmem_limit_bytes=VMEM_LIMIT),
        name="out_proj",
    )(o_a, o_b, xf, w_out, g)


def _rope_tables(positions, head_dim):
    rot = head_dim // ROPE_FRACTION
    half = rot // 2
    inv_freq = ROPE_THETA ** (-(jnp.arange(half, dtype=F32) * 2.0 / rot))
    n = positions.size
    per_row = LANES // half
    pos = jnp.repeat(positions.astype(F32).reshape(n // per_row, per_row), half, axis=1)
    ang = pos * jnp.tile(inv_freq, per_row)
    cos, sin = lax.optimization_barrier((jnp.cos(ang), jnp.sin(ang)))
    cos, sin = cos.reshape(n, half), sin.reshape(n, half)
    rest = head_dim - rot
    c = jnp.concatenate([cos, cos, jnp.ones((n, rest), F32)], axis=-1)
    s = jnp.concatenate([-sin, sin, jnp.zeros((n, rest), F32)], axis=-1)
    return jnp.tile(jnp.stack([c, s]), (1, 1, LANES // head_dim))


def _split_w_in(w):
    w_a = w[:, 0:WA_COLS].astype(BF16)
    w_b = w[:, WB_SRC0:WB_SRC0 + WB_COLS].astype(BF16)
    gab = w[:, WA_COLS:WB_SRC0]
    ikw = w[:, WB_SRC0 + WB_COLS:]
    pad = jnp.zeros((w.shape[0], LANES - gab.shape[1] - ikw.shape[1]), w.dtype)
    w_small = jnp.concatenate([ikw, gab, pad], axis=1).astype(BF16)
    return w_a, w_b, w_small


def kernel(x, positions, attn_norm_g, w_in, gdn_conv_w, gdn_a_log, gdn_dt_bias, gdn_norm_g,
           w_out, final_norm_g):
    batch, seq, d = x.shape
    assert d == D_MODEL and w_in.shape[0] == 1, "single-layer trunk with D_MODEL=2048 only"
    assert seq % max(GDN_TC, DSA_TQ, PROJ_TM) == 0 and (batch * seq) % OUT_TM == 0
    xf = x.reshape(batch * seq, d)

    w_a, w_b, w_small = _split_w_in(w_in[0])
    main, small, k_idx = _input_projection(xf, attn_norm_g[0].reshape(1, d), w_a, w_b, w_small,
                                           gdn_conv_w[0].astype(F32),
                                           _rope_tables(positions, DSA_HEAD_DIM),
                                           _rope_tables(positions, IDX_DIM), seq)

    def head_rows(p):
        return jnp.broadcast_to(jnp.tile(p, GDN_NCH)[:, None], (GDN_R, LANES))

    o_a = _gdn(main, small, head_rows(gdn_a_log[0]), head_rows(gdn_dt_bias[0]),
               gdn_norm_g[0].reshape(1, GDN_DV), batch, seq)
    o_b = _dsa(main, small, k_idx, batch, seq)

    out = _output_projection(o_a, o_b, xf, w_out[0].astype(BF16), final_norm_g.reshape(1, d))
    return out.reshape(batch, seq, d)
```

```python
import functools
import math

import jax
import jax.numpy as jnp
from jax import lax
from jax.experimental import pallas as pl
from jax.experimental.pallas import tpu as pltpu

F32 = jnp.float32
BF16 = jnp.bfloat16
I32 = jnp.int32

D_MODEL = 2048
GDN_HEADS = 8
GDN_DK = 128
GDN_DV = 128
CONV_WIDTH = 4
CHUNK = 64
DSA_HEADS = 8
DSA_KV_HEADS = 2
DSA_HEAD_DIM = 128
IDX_HEADS = 16
IDX_DIM = 64
TOPK_MAX = 256
ROPE_THETA = 500000.0
ROPE_FRACTION = 4
EPS = 1e-6

LANES = 128
SUBLANES = 8
PACKED_ROWS = 16
VMEM_LIMIT = 56 * 1024 * 1024

OFF_GQ, OFF_GK, OFF_GV, OFF_GZ = 0, 1024, 2048, 3072
OFF_AQ, OFF_AZ, OFF_IQ, OFF_AK, OFF_AV = 4096, 5120, 6144, 7168, 7424
MAIN_COLS = 7680
WA_COLS = 4096
WB_SRC0 = 4112
WB_COLS = 3584
WB_MAP = ((0, 1024, OFF_AQ), (1024, 256, OFF_AK), (1280, 256, OFF_AV),
          (1536, 1024, OFF_AZ), (2560, 1024, OFF_IQ))
SM_IK, SM_IW, SM_GA, SM_GB = 0, 64, 80, 88

INT_MIN = -2147483648
KEY_F32_LOWEST = INT_MIN + 0x00800000
NEG_BIG = -1e30


def _sigmoid(x):
    return 1.0 / (1.0 + jnp.exp(-x))


def _softplus(x):
    return jnp.maximum(x, 0.0) + jnp.log(1.0 + jnp.exp(-jnp.abs(x)))


def _dot(a, b):
    return jnp.dot(a, b, preferred_element_type=F32)


def _dot_nt(a, b):
    return lax.dot_general(a, b, (((1,), (1,)), ((), ())), preferred_element_type=F32)


def _dot_tn(a, b):
    return lax.dot_general(a, b, (((0,), (0,)), ((), ())), preferred_element_type=F32)


PROJ_TM = 256
PROJ_SEG = 512
HALF128 = DSA_HEAD_DIM // ROPE_FRACTION // 2
HALF64 = IDX_DIM // ROPE_FRACTION // 2
Q_SCALE = DSA_HEAD_DIM ** -0.5 * math.log2(math.e)


def _rope_slab(xs, tab_ref, half, head_dim):
    n = xs.shape[-1]
    lane = lax.broadcasted_iota(I32, xs.shape, 1) % head_dim
    partner = jnp.where(lane < half, pltpu.roll(xs, n - half, 1), pltpu.roll(xs, half, 1))
    return xs * tab_ref[0] + partner * tab_ref[1]


def _col_kind(col):
    if OFF_AQ <= col < OFF_AQ + DSA_HEADS * DSA_HEAD_DIM:
        return "q"
    if OFF_AK <= col < OFF_AK + DSA_KV_HEADS * DSA_HEAD_DIM:
        return "k"
    if OFF_IQ <= col < OFF_IQ + IDX_HEADS * IDX_DIM:
        return "i"
    return "plain"


def _wb_dest(col):
    for src0, width, dst0 in WB_MAP:
        if src0 <= col < src0 + width:
            return dst0 + col - src0
    raise ValueError(col)


def _proj_kernel(x_ref, g_ref, wa_ref, wb_ref, ws_ref, t128_ref, t64_ref,
                 main_ref, small_ref, kidx_ref):
    x = x_ref[...]
    ms = jnp.mean(x * x, axis=-1, keepdims=True)
    h = (x * lax.rsqrt(ms + EPS) * g_ref[...]).astype(BF16)
    s = _dot(h, ws_ref[...])
    lane = lax.broadcasted_iota(I32, s.shape, 1)
    s = jnp.where(lane < IDX_DIM, _rope_slab(s, t64_ref, HALF64, IDX_DIM), s)
    small_ref[...] = s
    kidx_ref[...] = s[:, SM_IK:SM_IK + IDX_DIM].astype(BF16)

    def segment(w_ref, c0, dest):
        acc = _dot(h, w_ref[:, c0:c0 + PROJ_SEG])
        for c in range(0, PROJ_SEG, LANES):
            val = acc[:, c:c + LANES]
            d0 = dest(c0 + c)
            kind = _col_kind(d0)
            if kind == "q":
                val = _rope_slab(val, t128_ref, HALF128, DSA_HEAD_DIM) * Q_SCALE
            elif kind == "k":
                val = _rope_slab(val, t128_ref, HALF128, DSA_HEAD_DIM)
            elif kind == "i":
                val = _rope_slab(val, t64_ref, HALF64, IDX_DIM)
            main_ref[:, d0:d0 + LANES] = val.astype(BF16)

    for c0 in range(0, WA_COLS, PROJ_SEG):
        segment(wa_ref, c0, lambda col: col)
    for c0 in range(0, WB_COLS, PROJ_SEG):
        segment(wb_ref, c0, _wb_dest)


def _input_projection(xf, g, w_a, w_b, w_small, tab128, tab64):
    n = xf.shape[0]
    tab_spec = pl.BlockSpec((2, PROJ_TM, LANES), lambda i: (0, i, 0))
    resident = pl.Buffered(1)
    return pl.pallas_call(
        _proj_kernel,
        grid=(n // PROJ_TM,),
        in_specs=[
            pl.BlockSpec((PROJ_TM, D_MODEL), lambda i: (i, 0)),
            pl.BlockSpec((1, D_MODEL), lambda i: (0, 0)),
            pl.BlockSpec((D_MODEL, WA_COLS), lambda i: (0, 0), pipeline_mode=resident),
            pl.BlockSpec((D_MODEL, WB_COLS), lambda i: (0, 0), pipeline_mode=resident),
            pl.BlockSpec((D_MODEL, LANES), lambda i: (0, 0), pipeline_mode=resident),
            tab_spec, tab_spec,
        ],
        out_specs=[
            pl.BlockSpec((PROJ_TM, MAIN_COLS), lambda i: (i, 0)),
            pl.BlockSpec((PROJ_TM, LANES), lambda i: (i, 0)),
            pl.BlockSpec((PROJ_TM, IDX_DIM), lambda i: (i, 0)),
        ],
        out_shape=[
            jax.ShapeDtypeStruct((n, MAIN_COLS), BF16),
            jax.ShapeDtypeStruct((n, LANES), F32),
            jax.ShapeDtypeStruct((n, IDX_DIM), BF16),
        ],
        compiler_params=pltpu.CompilerParams(
            dimension_semantics=("arbitrary",), vmem_limit_bytes=VMEM_LIMIT),
        name="in_proj",
    )(xf, g, w_a, w_b, w_small, tab128, tab64)


GDN_TC = 256
GDN_NCH = GDN_TC // CHUNK
GDN_HB = GDN_HEADS
GDN_R = GDN_HB * GDN_NCH
HALO = SUBLANES


def _gdn_kernel(q_ref, k_ref, v_ref, z_ref, cwq_ref, cwk_ref, cwv_ref, sm_ref,
                alog_ref, dtb_ref, ng_ref, o_ref, s_ref, carry_ref, rows_ref):
    n = pl.program_id(1)

    @pl.when(n == 0)
    def _():
        s_ref[...] = jnp.zeros_like(s_ref)
        carry_ref[...] = jnp.zeros_like(carry_ref)

    groups = GDN_TC // SUBLANES
    sub = lax.broadcasted_iota(I32, (groups, SUBLANES, GDN_HB * LANES), 1)

    def conv_silu(x_ref, w_ref, idx):
        x = x_ref[...].astype(F32)
        x3 = jnp.concatenate([carry_ref[idx], x], axis=0).reshape(groups + 1, SUBLANES, -1)
        carry_ref[idx] = x[GDN_TC - HALO:GDN_TC, :]
        w = w_ref[...]
        y = x * w[CONV_WIDTH - 1:CONV_WIDTH, :]
        for d in range(1, CONV_WIDTH):
            rot = pltpu.roll(x3, d, 1)
            shifted = jnp.where(sub < d, rot[0:groups], rot[1:groups + 1])
            y = y + shifted.reshape(GDN_TC, -1) * w[CONV_WIDTH - 1 - d:CONV_WIDTH - d, :]
        return y * _sigmoid(y)

    q = conv_silu(q_ref, cwq_ref, 0)
    k = conv_silu(k_ref, cwk_ref, 1)
    v = conv_silu(v_ref, cwv_ref, 2)

    def l2n(x):
        return x * lax.rsqrt(jnp.sum(x * x, axis=-1, keepdims=True) + EPS)

    def head(x, hl):
        return x[:, hl * LANES:(hl + 1) * LANES]

    qn = [l2n(head(q, hl)) * (GDN_DK ** -0.5) for hl in range(GDN_HB)]
    kn = [l2n(head(k, hl)) for hl in range(GDN_HB)]

    sm_t = sm_ref[...].T
    rows_ref[...] = jnp.zeros_like(rows_ref)
    for c in range(GDN_NCH):
        pos = slice(c * CHUNK, (c + 1) * CHUNK)
        rows_ref[0, c * GDN_HB:(c + 1) * GDN_HB, 0:CHUNK] = sm_t[SM_GA:SM_GA + GDN_HB, pos]
        rows_ref[1, c * GDN_HB:(c + 1) * GDN_HB, 0:CHUNK] = sm_t[SM_GB:SM_GB + GDN_HB, pos]
    g = -jnp.exp(alog_ref[...]) * _softplus(rows_ref[0] + dtb_ref[...])
    lane = lax.broadcasted_iota(I32, (GDN_R, LANES), 1)
    gc = g
    sh = 1
    while sh < CHUNK:
        gc = gc + jnp.where(lane >= sh, pltpu.roll(gc, sh, 1), 0.0)
        sh *= 2
    beta = _sigmoid(rows_ref[1])
    zpad = jnp.zeros((LANES - GDN_R, LANES), F32)
    gc_t = jnp.concatenate([gc, zpad], axis=0).T
    beta_t = jnp.concatenate([beta, zpad], axis=0).T

    ri = lax.broadcasted_iota(I32, (CHUNK, CHUNK), 0)
    ci = lax.broadcasted_iota(I32, (CHUNK, CHUNK), 1)
    incl = ri >= ci
    strict = ri > ci
    eye = jnp.where(ri == ci, 1.0, 0.0).astype(F32)
    ng = ng_ref[...]
    rs = range(GDN_R)

    qc, kc, vc, g_col, b_col, decay, eg = [], [], [], [], [], [], []
    for r in rs:
        c, hl = divmod(r, GDN_HB)
        rows = slice(c * CHUNK, (c + 1) * CHUNK)
        qc.append(qn[hl][rows])
        kc.append(kn[hl][rows])
        vc.append(head(v, hl)[rows])
        g_col.append(jnp.broadcast_to(gc_t[0:CHUNK, r:r + 1], (CHUNK, LANES)))
        b_col.append(jnp.broadcast_to(beta_t[0:CHUNK, r:r + 1], (CHUNK, LANES)))
        diff = g_col[r][:, 0:CHUNK] - gc[r:r + 1, 0:CHUNK]
        decay.append(jnp.where(incl, jnp.exp(jnp.where(incl, diff, 0.0)), 0.0))
        eg.append(jnp.exp(g_col[r]))

    qk = [_dot_nt(jnp.concatenate([qc[r], kc[r]], axis=0).astype(BF16), kc[r].astype(BF16))
          for r in rs]

    xk = [-jnp.where(strict, b_col[r][:, 0:CHUNK] * qk[r][CHUNK:] * decay[r], 0.0) for r in rs]
    tm = [eye + xk[r] for r in rs]
    xk = [_dot(xk[r].astype(BF16), xk[r].astype(BF16)) for r in rs]
    for _ in range(4):
        st = [_dot(jnp.concatenate([tm[r], xk[r]], axis=0).astype(BF16), xk[r].astype(BF16))
              for r in rs]
        tm = [tm[r] + st[r][0:CHUNK] for r in rs]
        xk = [st[r][CHUNK:] for r in rs]
    tm = [tm[r] + _dot(tm[r].astype(BF16), xk[r].astype(BF16)) for r in rs]

    wu = [_dot(tm[r].astype(BF16),
               jnp.concatenate([kc[r] * (b_col[r] * eg[r]), vc[r] * b_col[r]], axis=1).astype(BF16)
               ).astype(BF16) for r in rs]
    awu = [_dot((qk[r][0:CHUNK] * decay[r]).astype(BF16), wu[r]) for r in rs]
    g_last = [g_col[r][CHUNK - 1:CHUNK, :] for r in rs]
    mn = [_dot_tn((kc[r] * jnp.exp(g_last[r] - g_col[r])).astype(BF16), wu[r]) for r in rs]
    lhs = [jnp.concatenate([qc[r] * eg[r] - awu[r][:, 0:LANES], mn[r][:, 0:LANES]],
                           axis=0).astype(BF16) for r in rs]
    el = [jnp.exp(g_last[r]) for r in rs]

    s = [s_ref[hl] for hl in range(GDN_HB)]
    outs = {}
    for c in range(GDN_NCH):
        for hl in range(GDN_HB):
            r = c * GDN_HB + hl
            st = _dot(lhs[r], s[hl].astype(BF16))
            outs[r] = st[0:CHUNK] + awu[r][:, LANES:]
            s[hl] = s[hl] * el[r] - st[CHUNK:] + mn[r][:, LANES:]
    for hl in range(GDN_HB):
        s_ref[hl] = s[hl]
    for r in rs:
        c, hl = divmod(r, GDN_HB)
        o = outs[r]
        o = o * lax.rsqrt(jnp.mean(o * o, axis=-1, keepdims=True) + EPS) * ng
        z = z_ref[c * CHUNK:(c + 1) * CHUNK, hl * LANES:(hl + 1) * LANES].astype(F32)
        o_ref[c * CHUNK:(c + 1) * CHUNK, hl * LANES:(hl + 1) * LANES] = (
            o * (z * _sigmoid(z))).astype(BF16)


def _gdn(main, small, conv_w, alog_rows, dtb_rows, norm_g, batch, seq):
    nt = seq // GDN_TC
    hw = GDN_HB * LANES

    def tok(col0):
        return pl.BlockSpec((GDN_TC, hw), lambda b, n, c=col0 // hw: (b * nt + n, c))

    def cw(col0):
        return pl.BlockSpec((CONV_WIDTH, hw), lambda b, n, c=col0 // hw: (0, c))

    par_spec = pl.BlockSpec((GDN_R, LANES), lambda b, n: (0, 0))
    return pl.pallas_call(
        _gdn_kernel,
        grid=(batch, nt),
        in_specs=[tok(OFF_GQ), tok(OFF_GK), tok(OFF_GV), tok(OFF_GZ),
                  cw(0), cw(GDN_HEADS * GDN_DK), cw(2 * GDN_HEADS * GDN_DK),
                  pl.BlockSpec((GDN_TC, LANES), lambda b, n: (b * nt + n, 0)),
                  par_spec, par_spec,
                  pl.BlockSpec((1, LANES), lambda b, n: (0, 0))],
        out_specs=pl.BlockSpec((GDN_TC, hw), lambda b, n: (b * nt + n, 0)),
        out_shape=jax.ShapeDtypeStruct((batch * seq, GDN_HEADS * GDN_DV), BF16),
        scratch_shapes=[pltpu.VMEM((GDN_HB, GDN_DK, GDN_DV), F32),
                        pltpu.VMEM((3, HALO, hw), F32),
                        pltpu.VMEM((2, GDN_R, LANES), F32)],
        compiler_params=pltpu.CompilerParams(
            dimension_semantics=("arbitrary", "arbitrary"), vmem_limit_bytes=VMEM_LIMIT),
        name="gdn",
    )(main, main, main, main, conv_w, conv_w, conv_w, small, alog_rows, dtb_rows, norm_g)


DSA_TQ = 256
DSA_TK = 256
REP = DSA_HEADS // DSA_KV_HEADS


def _dsa_kernel(q_ref, qi_ref, sm_ref, k_ref, v_ref, ki_ref, z_ref, o_ref,
                keys_ref, keys16_ref, qih_ref, acc_ref, m_ref, l_ref, *, k_sel):
    i = pl.program_id(1)
    nkb = i + 1

    for hh in range(IDX_HEADS):
        qih_ref[hh] = qi_ref[:, hh * IDX_DIM:(hh + 1) * IDX_DIM]
    wt = sm_ref[...].T[SM_IW:SM_IW + IDX_HEADS, :] * (IDX_HEADS ** -0.5 * IDX_DIM ** -0.5)
    qpos = i * DSA_TQ + lax.broadcasted_iota(I32, (DSA_TK, DSA_TQ), 1)
    krow = lax.broadcasted_iota(I32, (DSA_TK, DSA_TQ), 0)

    def p1(kb, carry):
        off = pl.multiple_of(kb * DSA_TK, DSA_TK)
        kblk = ki_ref[pl.ds(off, DSA_TK), :]
        acc = jnp.zeros((DSA_TK, DSA_TQ), F32)
        for hh in range(IDX_HEADS):
            p = _dot_nt(kblk, qih_ref[hh])
            acc = acc + wt[hh:hh + 1, :] * jnp.maximum(p, 0.0)
        sc = jnp.where(kb * DSA_TK + krow <= qpos, acc, -jnp.inf)
        keys_ref[pl.ds(off, DSA_TK), :] = sc
        keys16_ref[pl.ds(off, DSA_TK), :] = sc.astype(BF16)
        return carry

    lax.fori_loop(0, nkb, p1, 0)

    def tree_sum(x, rows):
        parts = [x[r:r + rows, :] for r in range(0, x.shape[0], rows)]
        while len(parts) > 1:
            parts = [a + b for a, b in zip(parts[0::2], parts[1::2])]
        return parts[0]

    def key_to_bits(key, width):
        return key ^ ((key >> (width - 1)) & ((1 << (width - 1)) - 1))

    def p2a(p, u16):
        cand = u16 | lax.shift_left(jnp.int32(1), 15 - p)
        bits = lax.shift_left(key_to_bits(cand - 32768, 16), 16)
        c16 = pltpu.bitcast(bits, F32).astype(BF16)

        def cnt_body(kb, cnt):
            off = pl.multiple_of(kb * DSA_TK, DSA_TK)
            hit = jnp.where(keys16_ref[pl.ds(off, DSA_TK), :] >= c16,
                            jnp.ones((), BF16), jnp.zeros((), BF16))
            return cnt + tree_sum(hit, PACKED_ROWS).astype(F32)

        cnt = lax.fori_loop(0, nkb, cnt_body, jnp.zeros((PACKED_ROWS, DSA_TQ), F32))
        cnt = jnp.sum(cnt, axis=0, keepdims=True)
        return jnp.where(cnt >= k_sel, cand, u16)

    u16 = lax.fori_loop(0, 16, p2a, jnp.zeros((1, DSA_TQ), I32))
    coarse = lax.shift_left(key_to_bits(u16 - 32768, 16), 16)
    lo_key = key_to_bits(coarse, 32) - (1 << 15)

    def p2b(p, off_key):
        cand = off_key | lax.shift_left(jnp.int32(1), 16 - p)
        cand_f = pltpu.bitcast(key_to_bits(lo_key + cand, 32), F32)

        def cnt_body(kb, cnt):
            off = pl.multiple_of(kb * DSA_TK, DSA_TK)
            hit = jnp.where(keys_ref[pl.ds(off, DSA_TK), :] >= cand_f, 1, 0)
            return cnt + tree_sum(hit, SUBLANES)

        cnt = lax.fori_loop(0, nkb, cnt_body, jnp.zeros((SUBLANES, DSA_TQ), I32))
        cnt = jnp.sum(cnt, axis=0, keepdims=True)
        return jnp.where(cnt >= k_sel, cand, off_key)

    off_key = lax.fori_loop(0, 17, p2b, jnp.zeros((1, DSA_TQ), I32))
    tau = pltpu.bitcast(key_to_bits(jnp.maximum(lo_key + off_key, KEY_F32_LOWEST), 32), F32)

    m_ref[...] = jnp.full_like(m_ref, NEG_BIG)
    l_ref[...] = jnp.zeros_like(l_ref)
    acc_ref[...] = jnp.zeros_like(acc_ref)

    def p3(kb, carry):
        off = pl.multiple_of(kb * DSA_TK, DSA_TK)
        bias = jnp.where(keys_ref[pl.ds(off, DSA_TK), :] >= tau, 0.0, NEG_BIG).astype(BF16)
        heads = range(DSA_HEADS)
        kblk = [k_ref[pl.ds(off, DSA_TK), g * DSA_HEAD_DIM:(g + 1) * DSA_HEAD_DIM]
                for g in range(DSA_KV_HEADS)]
        vblk = [v_ref[pl.ds(off, DSA_TK), g * DSA_HEAD_DIM:(g + 1) * DSA_HEAD_DIM]
                for g in range(DSA_KV_HEADS)]
        s = [_dot_nt(kblk[hh // REP],
                     q_ref[:, hh * DSA_HEAD_DIM:(hh + 1) * DSA_HEAD_DIM]).astype(BF16) + bias
             for hh in heads]
        m_old = [m_ref[hh] for hh in heads]
        m_new = [jnp.maximum(m_old[hh], jnp.max(s[hh], axis=0, keepdims=True).astype(F32))
                 for hh in heads]
        alpha = [jnp.exp2(m_old[hh] - m_new[hh]) for hh in heads]
        p = [jnp.exp2(s[hh] - m_new[hh][0:1, :].astype(BF16)) for hh in heads]
        for hh in heads:
            part = tree_sum(p[hh], PACKED_ROWS).astype(F32)
            l_ref[hh] = alpha[hh] * l_ref[hh] + jnp.sum(part, axis=0, keepdims=True)
            m_ref[hh] = m_new[hh]
        pv = [_dot_tn(vblk[hh // REP], p[hh]) for hh in heads]
        for hh in heads:
            acc_ref[hh] = alpha[hh][0:1, :] * acc_ref[hh] + pv[hh]
        return carry

    lax.fori_loop(0, nkb, p3, 0)

    for hh in range(DSA_HEADS):
        o_t = acc_ref[hh] / l_ref[hh][0:1, :]
        z = z_ref[:, hh * DSA_HEAD_DIM:(hh + 1) * DSA_HEAD_DIM].astype(F32)
        o_ref[:, hh * DSA_HEAD_DIM:(hh + 1) * DSA_HEAD_DIM] = (
            o_t.T * (z * _sigmoid(z))).astype(BF16)


def _dsa(main, small, k_idx, batch, seq):
    nq = seq // DSA_TQ
    k_sel = min(TOPK_MAX, seq // 4)
    width = DSA_HEADS * DSA_HEAD_DIM
    kvw = DSA_KV_HEADS * DSA_HEAD_DIM

    def tokw(col0):
        return pl.BlockSpec((DSA_TQ, width), lambda b, i, c=col0 // width: (b * nq + i, c))

    return pl.pallas_call(
        functools.partial(_dsa_kernel, k_sel=k_sel),
        grid=(batch, nq),
        in_specs=[
            tokw(OFF_AQ), tokw(OFF_IQ),
            pl.BlockSpec((DSA_TQ, LANES), lambda b, i: (b * nq + i, 0)),
            pl.BlockSpec((seq, kvw), lambda b, i: (b, OFF_AK // kvw)),
            pl.BlockSpec((seq, kvw), lambda b, i: (b, OFF_AV // kvw)),
            pl.BlockSpec((seq, IDX_DIM), lambda b, i: (b, 0)),
            tokw(OFF_AZ),
        ],
        out_specs=pl.BlockSpec((DSA_TQ, width), lambda b, i: (b * nq + i, 0)),
        out_shape=jax.ShapeDtypeStruct((batch * seq, width), BF16),
        scratch_shapes=[
            pltpu.VMEM((seq, DSA_TQ), F32),
            pltpu.VMEM((seq, DSA_TQ), BF16),
            pltpu.VMEM((IDX_HEADS, DSA_TQ, IDX_DIM), BF16),
            pltpu.VMEM((DSA_HEADS, DSA_HEAD_DIM, DSA_TQ), F32),
            pltpu.VMEM((DSA_HEADS, SUBLANES, DSA_TQ), F32),
            pltpu.VMEM((DSA_HEADS, SUBLANES, DSA_TQ), F32),
        ],
        compiler_params=pltpu.CompilerParams(
            dimension_semantics=("arbitrary", "arbitrary"), vmem_limit_bytes=VMEM_LIMIT),
        name="dsa",
    )(main, main, small, main, main, k_idx, main)


OUT_TM = 512


def _out_kernel(oa_ref, ob_ref, x_ref, w_ref, g_ref, out_ref):
    wa = oa_ref.shape[1]
    mixed = _dot(oa_ref[...], w_ref[0:wa, :]) + _dot(ob_ref[...], w_ref[wa:, :])
    y = x_ref[...] + mixed
    ms = jnp.mean(y * y, axis=-1, keepdims=True)
    out_ref[...] = y * lax.rsqrt(ms + EPS) * g_ref[...]


def _output_projection(o_a, o_b, xf, w_out, g):
    n = xf.shape[0]
    wa, wb = o_a.shape[1], o_b.shape[1]
    return pl.pallas_call(
        _out_kernel,
        grid=(n // OUT_TM,),
        in_specs=[
            pl.BlockSpec((OUT_TM, wa), lambda i: (i, 0)),
            pl.BlockSpec((OUT_TM, wb), lambda i: (i, 0)),
            pl.BlockSpec((OUT_TM, D_MODEL), lambda i: (i, 0)),
            pl.BlockSpec((wa + wb, D_MODEL), lambda i: (0, 0)),
            pl.BlockSpec((1, D_MODEL), lambda i: (0, 0)),
        ],
        out_specs=pl.BlockSpec((OUT_TM, D_MODEL), lambda i: (i, 0)),
        out_shape=jax.ShapeDtypeStruct((n, D_MODEL), F32),
        compiler_params=pltpu.CompilerParams(
            dimension_semantics=("arbitrary",), vmem_limit_bytes=VMEM_LIMIT),
        name="out_proj",
    )(o_a, o_b, xf, w_out, g)


def _rope_tables(positions, head_dim):
    rot = head_dim // ROPE_FRACTION
    half = rot // 2
    lane = jnp.arange(LANES) % head_dim
    inv_freq = ROPE_THETA ** (-((lane % half).astype(F32) * 2.0 / rot))
    freq = jnp.where(lane < rot, inv_freq, 0.0)
    sign = jnp.where(lane < half, -1.0, 1.0).astype(F32)
    ang = positions.astype(F32).reshape(-1, 1) * freq
    return jnp.stack([jnp.cos(ang), jnp.sin(ang) * sign])


def _split_w_in(w):
    w_a = w[:, 0:WA_COLS].astype(BF16)
    w_b = w[:, WB_SRC0:WB_SRC0 + WB_COLS].astype(BF16)
    gab = w[:, WA_COLS:WB_SRC0]
    ikw = w[:, WB_SRC0 + WB_COLS:]
    pad = jnp.zeros((w.shape[0], LANES - gab.shape[1] - ikw.shape[1]), w.dtype)
    w_small = jnp.concatenate([ikw, gab, pad], axis=1).astype(BF16)
    return w_a, w_b, w_small


def kernel(x, positions, attn_norm_g, w_in, gdn_conv_w, gdn_a_log, gdn_dt_bias, gdn_norm_g,
           w_out, final_norm_g):
    batch, seq, d = x.shape
    assert d == D_MODEL and w_in.shape[0] == 1, "single-layer trunk with D_MODEL=2048 only"
    assert seq % max(GDN_TC, DSA_TQ) == 0 and (batch * seq) % max(PROJ_TM, OUT_TM) == 0
    xf = x.reshape(batch * seq, d)

    w_a, w_b, w_small = _split_w_in(w_in[0])
    main, small, k_idx = _input_projection(xf, attn_norm_g[0].reshape(1, d), w_a, w_b, w_small,
                                           _rope_tables(positions, DSA_HEAD_DIM),
                                           _rope_tables(positions, IDX_DIM))

    def head_rows(p):
        return jnp.broadcast_to(jnp.tile(p, GDN_NCH)[:, None], (GDN_R, LANES))

    o_a = _gdn(main, small, gdn_conv_w[0], head_rows(gdn_a_log[0]), head_rows(gdn_dt_bias[0]),
               gdn_norm_g[0].reshape(1, GDN_DV), batch, seq)
    o_b = _dsa(main, small, k_idx, batch, seq)

    out = _output_projection(o_a, o_b, xf, w_out[0].astype(BF16), final_norm_g.reshape(1, d))
    return out.reshape(batch, seq, d)
```

```python
import functools
import math

import jax
import jax.numpy as jnp
from jax import lax
from jax.experimental import pallas as pl
from jax.experimental.pallas import tpu as pltpu

F32 = jnp.float32
BF16 = jnp.bfloat16
I32 = jnp.int32

D_MODEL = 2048
GDN_HEADS = 8
GDN_DK = 128
GDN_DV = 128
CONV_WIDTH = 4
CHUNK = 64
DSA_HEADS = 8
DSA_KV_HEADS = 2
DSA_HEAD_DIM = 128
IDX_HEADS = 16
IDX_DIM = 64
TOPK_MAX = 256
ROPE_THETA = 500000.0
ROPE_FRACTION = 4
EPS = 1e-6

LANES = 128
SUBLANES = 8
PACKED_ROWS = 16
VMEM_LIMIT = 56 * 1024 * 1024

OFF_GQ, OFF_GK, OFF_GV, OFF_GZ = 0, 1024, 2048, 3072
OFF_AQ, OFF_AZ, OFF_IQ, OFF_AK, OFF_AV = 4096, 5120, 6144, 7168, 7424
MAIN_COLS = 7680
WA_COLS = 4096
WB_SRC0 = 4112
WB_COLS = 3584
WB_MAP = ((0, 1024, OFF_AQ), (1024, 256, OFF_AK), (1280, 256, OFF_AV),
          (1536, 1024, OFF_AZ), (2560, 1024, OFF_IQ))
SM_IK, SM_IW, SM_GA, SM_GB = 0, 64, 80, 88

INT_MIN = -2147483648
KEY_F32_LOWEST = INT_MIN + 0x00800000
NEG_BIG = -1e30


def _sigmoid(x):
    return 1.0 / (1.0 + jnp.exp(-x))


def _softplus(x):
    return jnp.maximum(x, 0.0) + jnp.log(1.0 + jnp.exp(-jnp.abs(x)))


def _dot(a, b):
    return jnp.dot(a, b, preferred_element_type=F32)


def _dot_nt(a, b):
    return lax.dot_general(a, b, (((1,), (1,)), ((), ())), preferred_element_type=F32)


def _dot_tn(a, b):
    return lax.dot_general(a, b, (((0,), (0,)), ((), ())), preferred_element_type=F32)


PROJ_TM = 256
PROJ_SEG = 512
HALF128 = DSA_HEAD_DIM // ROPE_FRACTION // 2
HALF64 = IDX_DIM // ROPE_FRACTION // 2
Q_SCALE = DSA_HEAD_DIM ** -0.5 * math.log2(math.e)


def _rope_slab(xs, tab_ref, half, head_dim):
    n = xs.shape[-1]
    lane = lax.broadcasted_iota(I32, xs.shape, 1) % head_dim
    partner = jnp.where(lane < half, pltpu.roll(xs, n - half, 1), pltpu.roll(xs, half, 1))
    return xs * tab_ref[0] + partner * tab_ref[1]


def _col_kind(col):
    if OFF_AQ <= col < OFF_AQ + DSA_HEADS * DSA_HEAD_DIM:
        return "q"
    if OFF_AK <= col < OFF_AK + DSA_KV_HEADS * DSA_HEAD_DIM:
        return "k"
    if OFF_IQ <= col < OFF_IQ + IDX_HEADS * IDX_DIM:
        return "i"
    return "plain"


def _wb_dest(col):
    for src0, width, dst0 in WB_MAP:
        if src0 <= col < src0 + width:
            return dst0 + col - src0
    raise ValueError(col)


def _proj_kernel(x_ref, g_ref, wa_ref, wb_ref, ws_ref, t128_ref, t64_ref,
                 main_ref, small_ref, kidx_ref):
    x = x_ref[...]
    ms = jnp.mean(x * x, axis=-1, keepdims=True)
    h = (x * lax.rsqrt(ms + EPS) * g_ref[...]).astype(BF16)
    s = _dot(h, ws_ref[...])
    lane = lax.broadcasted_iota(I32, s.shape, 1)
    s = jnp.where(lane < IDX_DIM, _rope_slab(s, t64_ref, HALF64, IDX_DIM), s)
    small_ref[...] = s
    kidx_ref[...] = s[:, SM_IK:SM_IK + IDX_DIM].astype(BF16)

    def segment(w_ref, c0, dest):
        acc = _dot(h, w_ref[:, c0:c0 + PROJ_SEG])
        for c in range(0, PROJ_SEG, LANES):
            val = acc[:, c:c + LANES]
            d0 = dest(c0 + c)
            kind = _col_kind(d0)
            if kind == "q":
                val = _rope_slab(val, t128_ref, HALF128, DSA_HEAD_DIM) * Q_SCALE
            elif kind == "k":
                val = _rope_slab(val, t128_ref, HALF128, DSA_HEAD_DIM)
            elif kind == "i":
                val = _rope_slab(val, t64_ref, HALF64, IDX_DIM)
            main_ref[:, d0:d0 + LANES] = val.astype(BF16)

    for c0 in range(0, WA_COLS, PROJ_SEG):
        segment(wa_ref, c0, lambda col: col)
    for c0 in range(0, WB_COLS, PROJ_SEG):
        segment(wb_ref, c0, _wb_dest)


def _input_projection(xf, g, w_a, w_b, w_small, tab128, tab64):
    n = xf.shape[0]
    tab_spec = pl.BlockSpec((2, PROJ_TM, LANES), lambda i: (0, i, 0))
    resident = pl.Buffered(1)
    return pl.pallas_call(
        _proj_kernel,
        grid=(n // PROJ_TM,),
        in_specs=[
            pl.BlockSpec((PROJ_TM, D_MODEL), lambda i: (i, 0)),
            pl.BlockSpec((1, D_MODEL), lambda i: (0, 0)),
            pl.BlockSpec((D_MODEL, WA_COLS), lambda i: (0, 0), pipeline_mode=resident),
            pl.BlockSpec((D_MODEL, WB_COLS), lambda i: (0, 0), pipeline_mode=resident),
            pl.BlockSpec((D_MODEL, LANES), lambda i: (0, 0), pipeline_mode=resident),
            tab_spec, tab_spec,
        ],
        out_specs=[
            pl.BlockSpec((PROJ_TM, MAIN_COLS), lambda i: (i, 0)),
            pl.BlockSpec((PROJ_TM, LANES), lambda i: (i, 0)),
            pl.BlockSpec((PROJ_TM, IDX_DIM), lambda i: (i, 0)),
        ],
        out_shape=[
            jax.ShapeDtypeStruct((n, MAIN_COLS), BF16),
            jax.ShapeDtypeStruct((n, LANES), F32),
            jax.ShapeDtypeStruct((n, IDX_DIM), BF16),
        ],
        compiler_params=pltpu.CompilerParams(
            dimension_semantics=("arbitrary",), vmem_limit_bytes=VMEM_LIMIT),
        name="in_proj",
    )(xf, g, w_a, w_b, w_small, tab128, tab64)


GDN_TC = 256
GDN_NCH = GDN_TC // CHUNK
GDN_HB = GDN_HEADS
GDN_R = GDN_HB * GDN_NCH
HALO = SUBLANES


def _gdn_kernel(q_ref, k_ref, v_ref, z_ref, cwq_ref, cwk_ref, cwv_ref, sm_ref,
                alog_ref, dtb_ref, ng_ref, o_ref, s_ref, carry_ref, rows_ref):
    n = pl.program_id(1)

    @pl.when(n == 0)
    def _():
        s_ref[...] = jnp.zeros_like(s_ref)
        carry_ref[...] = jnp.zeros_like(carry_ref)

    groups = GDN_TC // SUBLANES
    sub = lax.broadcasted_iota(I32, (groups, SUBLANES, GDN_HB * LANES), 1)

    def conv_silu(x_ref, w_ref, idx):
        x = x_ref[...].astype(F32)
        x3 = jnp.concatenate([carry_ref[idx], x], axis=0).reshape(groups + 1, SUBLANES, -1)
        carry_ref[idx] = x[GDN_TC - HALO:GDN_TC, :]
        w = w_ref[...]
        y = x * w[CONV_WIDTH - 1:CONV_WIDTH, :]
        for d in range(1, CONV_WIDTH):
            rot = pltpu.roll(x3, d, 1)
            shifted = jnp.where(sub < d, rot[0:groups], rot[1:groups + 1])
            y = y + shifted.reshape(GDN_TC, -1) * w[CONV_WIDTH - 1 - d:CONV_WIDTH - d, :]
        return y * _sigmoid(y)

    q = conv_silu(q_ref, cwq_ref, 0)
    k = conv_silu(k_ref, cwk_ref, 1)
    v = conv_silu(v_ref, cwv_ref, 2)

    def l2n(x):
        return x * lax.rsqrt(jnp.sum(x * x, axis=-1, keepdims=True) + EPS)

    def head(x, hl):
        return x[:, hl * LANES:(hl + 1) * LANES]

    qn = [l2n(head(q, hl)) * (GDN_DK ** -0.5) for hl in range(GDN_HB)]
    kn = [l2n(head(k, hl)) for hl in range(GDN_HB)]

    sm_t = sm_ref[...].T
    rows_ref[...] = jnp.zeros_like(rows_ref)
    for c in range(GDN_NCH):
        pos = slice(c * CHUNK, (c + 1) * CHUNK)
        rows_ref[0, c * GDN_HB:(c + 1) * GDN_HB, 0:CHUNK] = sm_t[SM_GA:SM_GA + GDN_HB, pos]
        rows_ref[1, c * GDN_HB:(c + 1) * GDN_HB, 0:CHUNK] = sm_t[SM_GB:SM_GB + GDN_HB, pos]
    g = -jnp.exp(alog_ref[...]) * _softplus(rows_ref[0] + dtb_ref[...])
    lane = lax.broadcasted_iota(I32, (GDN_R, LANES), 1)
    gc = g
    sh = 1
    while sh < CHUNK:
        gc = gc + jnp.where(lane >= sh, pltpu.roll(gc, sh, 1), 0.0)
        sh *= 2
    beta = _sigmoid(rows_ref[1])
    zpad = jnp.zeros((LANES - GDN_R, LANES), F32)
    gc_t = jnp.concatenate([gc, zpad], axis=0).T
    beta_t = jnp.concatenate([beta, zpad], axis=0).T

    ri = lax.broadcasted_iota(I32, (CHUNK, CHUNK), 0)
    ci = lax.broadcasted_iota(I32, (CHUNK, CHUNK), 1)
    incl = ri >= ci
    strict = ri > ci
    eye = jnp.where(ri == ci, 1.0, 0.0).astype(F32)
    ng = ng_ref[...]
    rs = range(GDN_R)

    qc, kc, vc, g_col, b_col, decay, eg = [], [], [], [], [], [], []
    for r in rs:
        c, hl = divmod(r, GDN_HB)
        rows = slice(c * CHUNK, (c + 1) * CHUNK)
        qc.append(qn[hl][rows])
        kc.append(kn[hl][rows])
        vc.append(head(v, hl)[rows])
        g_col.append(jnp.broadcast_to(gc_t[0:CHUNK, r:r + 1], (CHUNK, LANES)))
        b_col.append(jnp.broadcast_to(beta_t[0:CHUNK, r:r + 1], (CHUNK, LANES)))
        diff = g_col[r][:, 0:CHUNK] - gc[r:r + 1, 0:CHUNK]
        decay.append(jnp.where(incl, jnp.exp(jnp.where(incl, diff, 0.0)), 0.0))
        eg.append(jnp.exp(g_col[r]))

    qk = [_dot_nt(jnp.concatenate([qc[r], kc[r]], axis=0).astype(BF16), kc[r].astype(BF16))
          for r in rs]

    xk = [-jnp.where(strict, b_col[r][:, 0:CHUNK] * qk[r][CHUNK:] * decay[r], 0.0) for r in rs]
    tm = [eye + xk[r] for r in rs]
    xk = [_dot(xk[r].astype(BF16), xk[r].astype(BF16)) for r in rs]
    for _ in range(4):
        st = [_dot(jnp.concatenate([tm[r], xk[r]], axis=0).astype(BF16), xk[r].astype(BF16))
              for r in rs]
        tm = [tm[r] + st[r][0:CHUNK] for r in rs]
        xk = [st[r][CHUNK:] for r in rs]
    tm = [tm[r] + _dot(tm[r].astype(BF16), xk[r].astype(BF16)) for r in rs]

    wu = [_dot(tm[r].astype(BF16),
               jnp.concatenate([kc[r] * (b_col[r] * eg[r]), vc[r] * b_col[r]], axis=1).astype(BF16)
               ).astype(BF16) for r in rs]
    awu = [_dot((qk[r][0:CHUNK] * decay[r]).astype(BF16), wu[r]) for r in rs]
    g_last = [g_col[r][CHUNK - 1:CHUNK, :] for r in rs]
    mn = [_dot_tn((kc[r] * jnp.exp(g_last[r] - g_col[r])).astype(BF16), wu[r]) for r in rs]
    lhs = [jnp.concatenate([qc[r] * eg[r] - awu[r][:, 0:LANES], mn[r][:, 0:LANES]],
                           axis=0).astype(BF16) for r in rs]
    el = [jnp.exp(g_last[r]) for r in rs]

    s = [s_ref[hl] for hl in range(GDN_HB)]
    outs = {}
    for c in range(GDN_NCH):
        for hl in range(GDN_HB):
            r = c * GDN_HB + hl
            st = _dot(lhs[r], s[hl].astype(BF16))
            outs[r] = st[0:CHUNK] + awu[r][:, LANES:]
            s[hl] = s[hl] * el[r] - st[CHUNK:] + mn[r][:, LANES:]
    for hl in range(GDN_HB):
        s_ref[hl] = s[hl]
    for r in rs:
        c, hl = divmod(r, GDN_HB)
        o = outs[r]
        o = o * lax.rsqrt(jnp.mean(o * o, axis=-1, keepdims=True) + EPS) * ng
        z = z_ref[c * CHUNK:(c + 1) * CHUNK, hl * LANES:(hl + 1) * LANES].astype(F32)
        o_ref[c * CHUNK:(c + 1) * CHUNK, hl * LANES:(hl + 1) * LANES] = (
            o * (z * _sigmoid(z))).astype(BF16)


def _gdn(main, small, conv_w, alog_rows, dtb_rows, norm_g, batch, seq):
    nt = seq // GDN_TC
    hw = GDN_HB * LANES

    def tok(col0):
        return pl.BlockSpec((GDN_TC, hw), lambda b, n, c=col0 // hw: (b * nt + n, c))

    def cw(col0):
        return pl.BlockSpec((CONV_WIDTH, hw), lambda b, n, c=col0 // hw: (0, c))

    par_spec = pl.BlockSpec((GDN_R, LANES), lambda b, n: (0, 0))
    return pl.pallas_call(
        _gdn_kernel,
        grid=(batch, nt),
        in_specs=[tok(OFF_GQ), tok(OFF_GK), tok(OFF_GV), tok(OFF_GZ),
                  cw(0), cw(GDN_HEADS * GDN_DK), cw(2 * GDN_HEADS * GDN_DK),
                  pl.BlockSpec((GDN_TC, LANES), lambda b, n: (b * nt + n, 0)),
                  par_spec, par_spec,
                  pl.BlockSpec((1, LANES), lambda b, n: (0, 0))],
        out_specs=pl.BlockSpec((GDN_TC, hw), lambda b, n: (b * nt + n, 0)),
        out_shape=jax.ShapeDtypeStruct((batch * seq, GDN_HEADS * GDN_DV), BF16),
        scratch_shapes=[pltpu.VMEM((GDN_HB, GDN_DK, GDN_DV), F32),
                        pltpu.VMEM((3, HALO, hw), F32),
                        pltpu.VMEM((2, GDN_R, LANES), F32)],
        compiler_params=pltpu.CompilerParams(
            dimension_semantics=("arbitrary", "arbitrary"), vmem_limit_bytes=VMEM_LIMIT),
        name="gdn",
    )(main, main, main, main, conv_w, conv_w, conv_w, small, alog_rows, dtb_rows, norm_g)


DSA_TQ = 256
DSA_TK = 256
DSA_TK3 = 512
REP = DSA_HEADS // DSA_KV_HEADS


def _dsa_kernel(q_ref, qi_ref, sm_ref, k_ref, v_ref, ki_ref, z_ref, o_ref,
                keys_ref, keys16_ref, qih_ref, acc_ref, m_ref, l_ref, *, k_sel):
    i = pl.program_id(1)
    nkb = i + 1

    for hh in range(IDX_HEADS):
        qih_ref[hh] = qi_ref[:, hh * IDX_DIM:(hh + 1) * IDX_DIM]
    wt = sm_ref[...].T[SM_IW:SM_IW + IDX_HEADS, :] * (IDX_HEADS ** -0.5 * IDX_DIM ** -0.5)
    qpos = i * DSA_TQ + lax.broadcasted_iota(I32, (DSA_TK, DSA_TQ), 1)
    krow = lax.broadcasted_iota(I32, (DSA_TK, DSA_TQ), 0)

    def p1(kb, carry):
        off = pl.multiple_of(kb * DSA_TK, DSA_TK)
        kblk = ki_ref[pl.ds(off, DSA_TK), :]
        acc = jnp.zeros((DSA_TK, DSA_TQ), F32)
        for hh in range(IDX_HEADS):
            p = _dot_nt(kblk, qih_ref[hh])
            acc = acc + wt[hh:hh + 1, :] * jnp.maximum(p, 0.0)
        sc = jnp.where(kb * DSA_TK + krow <= qpos, acc, -jnp.inf)
        keys_ref[pl.ds(off, DSA_TK), :] = sc
        keys16_ref[pl.ds(off, DSA_TK), :] = sc.astype(BF16)
        return carry

    lax.fori_loop(0, nkb, p1, 0)

    def tree_sum(x, rows):
        parts = [x[r:r + rows, :] for r in range(0, x.shape[0], rows)]
        while len(parts) > 1:
            parts = [a + b for a, b in zip(parts[0::2], parts[1::2])]
        return parts[0]

    def key_to_bits(key, width):
        return key ^ ((key >> (width - 1)) & ((1 << (width - 1)) - 1))

    def p2a(p, u16):
        cand = u16 | lax.shift_left(jnp.int32(1), 15 - p)
        bits = lax.shift_left(key_to_bits(cand - 32768, 16), 16)
        c16 = pltpu.bitcast(bits, F32).astype(BF16)

        def cnt_body(kb, cnt):
            off = pl.multiple_of(kb * DSA_TK, DSA_TK)
            hit = jnp.where(keys16_ref[pl.ds(off, DSA_TK), :] >= c16,
                            jnp.ones((), BF16), jnp.zeros((), BF16))
            return cnt + tree_sum(hit, PACKED_ROWS).astype(F32)

        cnt = lax.fori_loop(0, nkb, cnt_body, jnp.zeros((PACKED_ROWS, DSA_TQ), F32))
        cnt = jnp.sum(cnt, axis=0, keepdims=True)
        return jnp.where(cnt >= k_sel, cand, u16)

    u16 = lax.fori_loop(0, 16, p2a, jnp.zeros((1, DSA_TQ), I32))
    coarse = lax.shift_left(key_to_bits(u16 - 32768, 16), 16)
    lo_key = key_to_bits(coarse, 32) - (1 << 15)

    def p2b(p, off_key):
        cand = off_key | lax.shift_left(jnp.int32(1), 16 - p)
        cand_f = pltpu.bitcast(key_to_bits(lo_key + cand, 32), F32)

        def cnt_body(kb, cnt):
            off = pl.multiple_of(kb * DSA_TK, DSA_TK)
            hit = jnp.where(keys_ref[pl.ds(off, DSA_TK), :] >= cand_f, 1, 0)
            return cnt + tree_sum(hit, SUBLANES)

        cnt = lax.fori_loop(0, nkb, cnt_body, jnp.zeros((SUBLANES, DSA_TQ), I32))
        cnt = jnp.sum(cnt, axis=0, keepdims=True)
        return jnp.where(cnt >= k_sel, cand, off_key)

    off_key = lax.fori_loop(0, 17, p2b, jnp.zeros((1, DSA_TQ), I32))
    tau = pltpu.bitcast(key_to_bits(jnp.maximum(lo_key + off_key, KEY_F32_LOWEST), 32), F32)

    m_ref[...] = jnp.full_like(m_ref, NEG_BIG)
    l_ref[...] = jnp.zeros_like(l_ref)
    acc_ref[...] = jnp.zeros_like(acc_ref)

    def attend(off, width):
        bias = jnp.where(keys_ref[pl.ds(off, width), :] >= tau, 0.0, NEG_BIG).astype(BF16)
        heads = range(DSA_HEADS)
        kblk = [k_ref[pl.ds(off, width), g * DSA_HEAD_DIM:(g + 1) * DSA_HEAD_DIM]
                for g in range(DSA_KV_HEADS)]
        vblk = [v_ref[pl.ds(off, width), g * DSA_HEAD_DIM:(g + 1) * DSA_HEAD_DIM]
                for g in range(DSA_KV_HEADS)]
        s = [_dot_nt(kblk[hh // REP],
                     q_ref[:, hh * DSA_HEAD_DIM:(hh + 1) * DSA_HEAD_DIM]).astype(BF16) + bias
             for hh in heads]
        m_old = [m_ref[hh] for hh in heads]
        m_new = [jnp.maximum(m_old[hh], jnp.max(s[hh], axis=0, keepdims=True).astype(F32))
                 for hh in heads]
        alpha = [jnp.exp2(m_old[hh] - m_new[hh]) for hh in heads]
        p = [jnp.exp2(s[hh] - m_new[hh][0:1, :].astype(BF16)) for hh in heads]
        for hh in heads:
            part = tree_sum(p[hh], PACKED_ROWS).astype(F32)
            l_ref[hh] = alpha[hh] * l_ref[hh] + jnp.sum(part, axis=0, keepdims=True)
            m_ref[hh] = m_new[hh]
        pv = [_dot_tn(vblk[hh // REP], p[hh]) for hh in heads]
        for hh in heads:
            acc_ref[hh] = alpha[hh][0:1, :] * acc_ref[hh] + pv[hh]

    def p3(kb, carry):
        attend(pl.multiple_of(kb * DSA_TK3, DSA_TK3), DSA_TK3)
        return carry

    wide = DSA_TK3 // DSA_TK
    lax.fori_loop(0, nkb // wide, p3, 0)
    for rem in range(1, wide):
        @pl.when(nkb % wide >= rem)
        def _():
            attend(pl.multiple_of((nkb // wide * wide + rem - 1) * DSA_TK, DSA_TK), DSA_TK)

    for hh in range(DSA_HEADS):
        o_t = acc_ref[hh] / l_ref[hh][0:1, :]
        z = z_ref[:, hh * DSA_HEAD_DIM:(hh + 1) * DSA_HEAD_DIM].astype(F32)
        o_ref[:, hh * DSA_HEAD_DIM:(hh + 1) * DSA_HEAD_DIM] = (
            o_t.T * (z * _sigmoid(z))).astype(BF16)


def _dsa(main, small, k_idx, batch, seq):
    nq = seq // DSA_TQ
    k_sel = min(TOPK_MAX, seq // 4)
    width = DSA_HEADS * DSA_HEAD_DIM
    kvw = DSA_KV_HEADS * DSA_HEAD_DIM

    def tokw(col0):
        return pl.BlockSpec((DSA_TQ, width), lambda b, i, c=col0 // width: (b * nq + i, c))

    return pl.pallas_call(
        functools.partial(_dsa_kernel, k_sel=k_sel),
        grid=(batch, nq),
        in_specs=[
            tokw(OFF_AQ), tokw(OFF_IQ),
            pl.BlockSpec((DSA_TQ, LANES), lambda b, i: (b * nq + i, 0)),
            pl.BlockSpec((seq, kvw), lambda b, i: (b, OFF_AK // kvw)),
            pl.BlockSpec((seq, kvw), lambda b, i: (b, OFF_AV // kvw)),
            pl.BlockSpec((seq, IDX_DIM), lambda b, i: (b, 0)),
            tokw(OFF_AZ),
        ],
        out_specs=pl.BlockSpec((DSA_TQ, width), lambda b, i: (b * nq + i, 0)),
        out_shape=jax.ShapeDtypeStruct((batch * seq, width), BF16),
        scratch_shapes=[
            pltpu.VMEM((seq, DSA_TQ), F32),
            pltpu.VMEM((seq, DSA_TQ), BF16),
            pltpu.VMEM((IDX_HEADS, DSA_TQ, IDX_DIM), BF16),
            pltpu.VMEM((DSA_HEADS, DSA_HEAD_DIM, DSA_TQ), F32),
            pltpu.VMEM((DSA_HEADS, SUBLANES, DSA_TQ), F32),
            pltpu.VMEM((DSA_HEADS, SUBLANES, DSA_TQ), F32),
        ],
        compiler_params=pltpu.CompilerParams(
            dimension_semantics=("arbitrary", "arbitrary"), vmem_limit_bytes=VMEM_LIMIT),
        name="dsa",
    )(main, main, small, main, main, k_idx, main)


OUT_TM = 512


def _out_kernel(oa_ref, ob_ref, x_ref, w_ref, g_ref, out_ref):
    wa = oa_ref.shape[1]
    mixed = _dot(oa_ref[...], w_ref[0:wa, :]) + _dot(ob_ref[...], w_ref[wa:, :])
    y = x_ref[...] + mixed
    ms = jnp.mean(y * y, axis=-1, keepdims=True)
    out_ref[...] = y * lax.rsqrt(ms + EPS) * g_ref[...]


def _output_projection(o_a, o_b, xf, w_out, g):
    n = xf.shape[0]
    wa, wb = o_a.shape[1], o_b.shape[1]
    return pl.pallas_call(
        _out_kernel,
        grid=(n // OUT_TM,),
        in_specs=[
            pl.BlockSpec((OUT_TM, wa), lambda i: (i, 0)),
            pl.BlockSpec((OUT_TM, wb), lambda i: (i, 0)),
            pl.BlockSpec((OUT_TM, D_MODEL), lambda i: (i, 0)),
            pl.BlockSpec((wa + wb, D_MODEL), lambda i: (0, 0)),
            pl.BlockSpec((1, D_MODEL), lambda i: (0, 0)),
        ],
        out_specs=pl.BlockSpec((OUT_TM, D_MODEL), lambda i: (i, 0)),
        out_shape=jax.ShapeDtypeStruct((n, D_MODEL), F32),
        compiler_params=pltpu.CompilerParams(
            dimension_semantics=("arbitrary",), vmem_limit_bytes=VMEM_LIMIT),
        name="out_proj",
    )(o_a, o_b, xf, w_out, g)


def _rope_tables(positions, head_dim):
    rot = head_dim // ROPE_FRACTION
    half = rot // 2
    lane = jnp.arange(LANES) % head_dim
    inv_freq = ROPE_THETA ** (-((lane % half).astype(F32) * 2.0 / rot))
    freq = jnp.where(lane < rot, inv_freq, 0.0)
    sign = jnp.where(lane < half, -1.0, 1.0).astype(F32)
    ang = positions.astype(F32).reshape(-1, 1) * freq
    return jnp.stack([jnp.cos(ang), jnp.sin(ang) * sign])


def _split_w_in(w):
    w_a = w[:, 0:WA_COLS].astype(BF16)
    w_b = w[:, WB_SRC0:WB_SRC0 + WB_COLS].astype(BF16)
    gab = w[:, WA_COLS:WB_SRC0]
    ikw = w[:, WB_SRC0 + WB_COLS:]
    pad = jnp.zeros((w.shape[0], LANES - gab.shape[1] - ikw.shape[1]), w.dtype)
    w_small = jnp.concatenate([ikw, gab, pad], axis=1).astype(BF16)
    return w_a, w_b, w_small


def kernel(x, positions, attn_norm_g, w_in, gdn_conv_w, gdn_a_log, gdn_dt_bias, gdn_norm_g,
           w_out, final_norm_g):
    batch, seq, d = x.shape
    assert d == D_MODEL and w_in.shape[0] == 1, "single-layer trunk with D_MODEL=2048 only"
    assert seq % max(GDN_TC, DSA_TQ) == 0 and (batch * seq) % max(PROJ_TM, OUT_TM) == 0
    xf = x.reshape(batch * seq, d)

    w_a, w_b, w_small = _split_w_in(w_in[0])
    main, small, k_idx = _input_projection(xf, attn_norm_g[0].reshape(1, d), w_a, w_b, w_small,
                                           _rope_tables(positions, DSA_HEAD_DIM),
                                           _rope_tables(positions, IDX_DIM))

    def head_rows(p):
        return jnp.broadcast_to(jnp.tile(p, GDN_NCH)[:, None], (GDN_R, LANES))

    o_a = _gdn(main, small, gdn_conv_w[0], head_rows(gdn_a_log[0]), head_rows(gdn_dt_bias[0]),
               gdn_norm_g[0].reshape(1, GDN_DV), batch, seq)
    o_b = _dsa(main, small, k_idx, batch, seq)

    out = _output_projection(o_a, o_b, xf, w_out[0].astype(BF16), final_norm_g.reshape(1, d))
    return out.reshape(batch, seq, d)
```

```python
import functools
import math

import jax
import jax.numpy as jnp
from jax import lax
from jax.experimental import pallas as pl
from jax.experimental.pallas import tpu as pltpu

F32 = jnp.float32
BF16 = jnp.bfloat16
I32 = jnp.int32

D_MODEL = 2048
GDN_HEADS = 8
GDN_DK = 128
GDN_DV = 128
CONV_WIDTH = 4
CHUNK = 64
DSA_HEADS = 8
DSA_KV_HEADS = 2
DSA_HEAD_DIM = 128
IDX_HEADS = 16
IDX_DIM = 64
TOPK_MAX = 256
ROPE_THETA = 500000.0
ROPE_FRACTION = 4
EPS = 1e-6

LANES = 128
SUBLANES = 8
PACKED_ROWS = 16
VMEM_LIMIT = 56 * 1024 * 1024

OFF_GQ, OFF_GK, OFF_GV, OFF_GZ = 0, 1024, 2048, 3072
OFF_AQ, OFF_AZ, OFF_IQ, OFF_AK, OFF_AV = 4096, 5120, 6144, 7168, 7424
MAIN_COLS = 7680
WA_COLS = 4096
WB_SRC0 = 4112
WB_COLS = 3584
WB_MAP = ((0, 1024, OFF_AQ), (1024, 256, OFF_AK), (1280, 256, OFF_AV),
          (1536, 1024, OFF_AZ), (2560, 1024, OFF_IQ))
SM_IK, SM_IW, SM_GA, SM_GB = 0, 64, 80, 88

INT_MIN = -2147483648
KEY_F32_LOWEST = INT_MIN + 0x00800000
NEG_BIG = -1e30


def _sigmoid(x):
    return 1.0 / (1.0 + jnp.exp(-x))


def _softplus(x):
    return jnp.maximum(x, 0.0) + jnp.log(1.0 + jnp.exp(-jnp.abs(x)))


def _dot(a, b):
    return jnp.dot(a, b, preferred_element_type=F32)


def _dot_nt(a, b):
    return lax.dot_general(a, b, (((1,), (1,)), ((), ())), preferred_element_type=F32)


def _dot_tn(a, b):
    return lax.dot_general(a, b, (((0,), (0,)), ((), ())), preferred_element_type=F32)


PROJ_TM = 256
PROJ_SEG = 512
HALF128 = DSA_HEAD_DIM // ROPE_FRACTION // 2
HALF64 = IDX_DIM // ROPE_FRACTION // 2
Q_SCALE = DSA_HEAD_DIM ** -0.5 * math.log2(math.e)


def _rope_slab(xs, tab_ref, half, head_dim):
    n = xs.shape[-1]
    lane = lax.broadcasted_iota(I32, xs.shape, 1) % head_dim
    partner = jnp.where(lane < half, pltpu.roll(xs, n - half, 1), pltpu.roll(xs, half, 1))
    return xs * tab_ref[0] + partner * tab_ref[1]


def _col_kind(col):
    if OFF_AQ <= col < OFF_AQ + DSA_HEADS * DSA_HEAD_DIM:
        return "q"
    if OFF_AK <= col < OFF_AK + DSA_KV_HEADS * DSA_HEAD_DIM:
        return "k"
    if OFF_IQ <= col < OFF_IQ + IDX_HEADS * IDX_DIM:
        return "i"
    return "plain"


def _wb_dest(col):
    for src0, width, dst0 in WB_MAP:
        if src0 <= col < src0 + width:
            return dst0 + col - src0
    raise ValueError(col)


def _proj_kernel(x_ref, g_ref, wa_ref, wb_ref, ws_ref, t128_ref, t64_ref,
                 main_ref, small_ref, kidx_ref):
    x = x_ref[...]
    ms = jnp.mean(x * x, axis=-1, keepdims=True)
    h = (x * lax.rsqrt(ms + EPS) * g_ref[...]).astype(BF16)
    s = _dot(h, ws_ref[...])
    lane = lax.broadcasted_iota(I32, s.shape, 1)
    s = jnp.where(lane < IDX_DIM, _rope_slab(s, t64_ref, HALF64, IDX_DIM), s)
    small_ref[...] = s
    kidx_ref[...] = s[:, SM_IK:SM_IK + IDX_DIM].astype(BF16)

    def segment(w_ref, c0, dest):
        acc = _dot(h, w_ref[:, c0:c0 + PROJ_SEG])
        for c in range(0, PROJ_SEG, LANES):
            val = acc[:, c:c + LANES]
            d0 = dest(c0 + c)
            kind = _col_kind(d0)
            if kind == "q":
                val = _rope_slab(val, t128_ref, HALF128, DSA_HEAD_DIM) * Q_SCALE
            elif kind == "k":
                val = _rope_slab(val, t128_ref, HALF128, DSA_HEAD_DIM)
            elif kind == "i":
                val = _rope_slab(val, t64_ref, HALF64, IDX_DIM)
            main_ref[:, d0:d0 + LANES] = val.astype(BF16)

    for c0 in range(0, WA_COLS, PROJ_SEG):
        segment(wa_ref, c0, lambda col: col)
    for c0 in range(0, WB_COLS, PROJ_SEG):
        segment(wb_ref, c0, _wb_dest)


def _input_projection(xf, g, w_a, w_b, w_small, tab128, tab64):
    n = xf.shape[0]
    tab_spec = pl.BlockSpec((2, PROJ_TM, LANES), lambda i: (0, i, 0))
    resident = pl.Buffered(1)
    return pl.pallas_call(
        _proj_kernel,
        grid=(n // PROJ_TM,),
        in_specs=[
            pl.BlockSpec((PROJ_TM, D_MODEL), lambda i: (i, 0)),
            pl.BlockSpec((1, D_MODEL), lambda i: (0, 0)),
            pl.BlockSpec((D_MODEL, WA_COLS), lambda i: (0, 0), pipeline_mode=resident),
            pl.BlockSpec((D_MODEL, WB_COLS), lambda i: (0, 0), pipeline_mode=resident),
            pl.BlockSpec((D_MODEL, LANES), lambda i: (0, 0), pipeline_mode=resident),
            tab_spec, tab_spec,
        ],
        out_specs=[
            pl.BlockSpec((PROJ_TM, MAIN_COLS), lambda i: (i, 0)),
            pl.BlockSpec((PROJ_TM, LANES), lambda i: (i, 0)),
            pl.BlockSpec((PROJ_TM, IDX_DIM), lambda i: (i, 0)),
        ],
        out_shape=[
            jax.ShapeDtypeStruct((n, MAIN_COLS), BF16),
            jax.ShapeDtypeStruct((n, LANES), F32),
            jax.ShapeDtypeStruct((n, IDX_DIM), BF16),
        ],
        compiler_params=pltpu.CompilerParams(
            dimension_semantics=("arbitrary",), vmem_limit_bytes=VMEM_LIMIT),
        name="in_proj",
    )(xf, g, w_a, w_b, w_small, tab128, tab64)


GDN_TC = 256
GDN_NCH = GDN_TC // CHUNK
GDN_HB = GDN_HEADS
GDN_R = GDN_HB * GDN_NCH
HALO = SUBLANES


def _gdn_kernel(q_ref, k_ref, v_ref, z_ref, cwq_ref, cwk_ref, cwv_ref, sm_ref,
                alog_ref, dtb_ref, ng_ref, o_ref, s_ref, carry_ref, rows_ref):
    n = pl.program_id(1)

    @pl.when(n == 0)
    def _():
        s_ref[...] = jnp.zeros_like(s_ref)
        carry_ref[...] = jnp.zeros_like(carry_ref)

    groups = GDN_TC // SUBLANES
    sub = lax.broadcasted_iota(I32, (groups, SUBLANES, GDN_HB * LANES), 1)

    def conv_silu(x_ref, w_ref, idx):
        x = x_ref[...].astype(F32)
        x3 = jnp.concatenate([carry_ref[idx], x], axis=0).reshape(groups + 1, SUBLANES, -1)
        carry_ref[idx] = x[GDN_TC - HALO:GDN_TC, :]
        w = w_ref[...]
        y = x * w[CONV_WIDTH - 1:CONV_WIDTH, :]
        for d in range(1, CONV_WIDTH):
            rot = pltpu.roll(x3, d, 1)
            shifted = jnp.where(sub < d, rot[0:groups], rot[1:groups + 1])
            y = y + shifted.reshape(GDN_TC, -1) * w[CONV_WIDTH - 1 - d:CONV_WIDTH - d, :]
        return y * _sigmoid(y)

    q = conv_silu(q_ref, cwq_ref, 0)
    k = conv_silu(k_ref, cwk_ref, 1)
    v = conv_silu(v_ref, cwv_ref, 2)

    def l2n(x):
        return x * lax.rsqrt(jnp.sum(x * x, axis=-1, keepdims=True) + EPS)

    def head(x, hl):
        return x[:, hl * LANES:(hl + 1) * LANES]

    qn = [l2n(head(q, hl)) * (GDN_DK ** -0.5) for hl in range(GDN_HB)]
    kn = [l2n(head(k, hl)) for hl in range(GDN_HB)]

    sm_t = sm_ref[...].T
    rows_ref[...] = jnp.zeros_like(rows_ref)
    for c in range(GDN_NCH):
        pos = slice(c * CHUNK, (c + 1) * CHUNK)
        rows_ref[0, c * GDN_HB:(c + 1) * GDN_HB, 0:CHUNK] = sm_t[SM_GA:SM_GA + GDN_HB, pos]
        rows_ref[1, c * GDN_HB:(c + 1) * GDN_HB, 0:CHUNK] = sm_t[SM_GB:SM_GB + GDN_HB, pos]
    g = -jnp.exp(alog_ref[...]) * _softplus(rows_ref[0] + dtb_ref[...])
    lane = lax.broadcasted_iota(I32, (GDN_R, LANES), 1)
    gc = g
    sh = 1
    while sh < CHUNK:
        gc = gc + jnp.where(lane >= sh, pltpu.roll(gc, sh, 1), 0.0)
        sh *= 2
    beta = _sigmoid(rows_ref[1])
    zpad = jnp.zeros((LANES - GDN_R, LANES), F32)
    gc_t = jnp.concatenate([gc, zpad], axis=0).T
    beta_t = jnp.concatenate([beta, zpad], axis=0).T

    ri = lax.broadcasted_iota(I32, (CHUNK, CHUNK), 0)
    ci = lax.broadcasted_iota(I32, (CHUNK, CHUNK), 1)
    incl = ri >= ci
    strict = ri > ci
    eye = jnp.where(ri == ci, 1.0, 0.0).astype(F32)
    ng = ng_ref[...]
    rs = range(GDN_R)

    qc, kc, vc, g_col, b_col, decay, eg = [], [], [], [], [], [], []
    for r in rs:
        c, hl = divmod(r, GDN_HB)
        rows = slice(c * CHUNK, (c + 1) * CHUNK)
        qc.append(qn[hl][rows])
        kc.append(kn[hl][rows])
        vc.append(head(v, hl)[rows])
        g_col.append(jnp.broadcast_to(gc_t[0:CHUNK, r:r + 1], (CHUNK, LANES)))
        b_col.append(jnp.broadcast_to(beta_t[0:CHUNK, r:r + 1], (CHUNK, LANES)))
        diff = g_col[r][:, 0:CHUNK] - gc[r:r + 1, 0:CHUNK]
        decay.append(jnp.where(incl, jnp.exp(jnp.where(incl, diff, 0.0)), 0.0))
        eg.append(jnp.exp(g_col[r]))

    qk = [_dot_nt(jnp.concatenate([qc[r], kc[r]], axis=0).astype(BF16), kc[r].astype(BF16))
          for r in rs]

    xk = [-jnp.where(strict, b_col[r][:, 0:CHUNK] * qk[r][CHUNK:] * decay[r], 0.0) for r in rs]
    tm = [eye + xk[r] for r in rs]
    xk = [_dot(xk[r].astype(BF16), xk[r].astype(BF16)) for r in rs]
    for _ in range(4):
        st = [_dot(jnp.concatenate([tm[r], xk[r]], axis=0).astype(BF16), xk[r].astype(BF16))
              for r in rs]
        tm = [tm[r] + st[r][0:CHUNK] for r in rs]
        xk = [st[r][CHUNK:] for r in rs]
    tm = [tm[r] + _dot(tm[r].astype(BF16), xk[r].astype(BF16)) for r in rs]

    wu = [_dot(tm[r].astype(BF16),
               jnp.concatenate([kc[r] * (b_col[r] * eg[r]), vc[r] * b_col[r]], axis=1).astype(BF16)
               ).astype(BF16) for r in rs]
    awu = [_dot((qk[r][0:CHUNK] * decay[r]).astype(BF16), wu[r]) for r in rs]
    g_last = [g_col[r][CHUNK - 1:CHUNK, :] for r in rs]
    mn = [_dot_tn((kc[r] * jnp.exp(g_last[r] - g_col[r])).astype(BF16), wu[r]) for r in rs]
    lhs = [jnp.concatenate([qc[r] * eg[r] - awu[r][:, 0:LANES], mn[r][:, 0:LANES]],
                           axis=0).astype(BF16) for r in rs]
    el = [jnp.exp(g_last[r]) for r in rs]

    s = [s_ref[hl] for hl in range(GDN_HB)]
    outs = {}
    for c in range(GDN_NCH):
        for hl in range(GDN_HB):
            r = c * GDN_HB + hl
            st = _dot(lhs[r], s[hl].astype(BF16))
            outs[r] = st[0:CHUNK] + awu[r][:, LANES:]
            s[hl] = s[hl] * el[r] - st[CHUNK:] + mn[r][:, LANES:]
    for hl in range(GDN_HB):
        s_ref[hl] = s[hl]
    for r in rs:
        c, hl = divmod(r, GDN_HB)
        o = outs[r]
        o = o * lax.rsqrt(jnp.mean(o * o, axis=-1, keepdims=True) + EPS) * ng
        z = z_ref[c * CHUNK:(c + 1) * CHUNK, hl * LANES:(hl + 1) * LANES].astype(F32)
        o_ref[c * CHUNK:(c + 1) * CHUNK, hl * LANES:(hl + 1) * LANES] = (
            o * (z * _sigmoid(z))).astype(BF16)


def _gdn(main, small, conv_w, alog_rows, dtb_rows, norm_g, batch, seq):
    nt = seq // GDN_TC
    hw = GDN_HB * LANES

    def tok(col0):
        return pl.BlockSpec((GDN_TC, hw), lambda b, n, c=col0 // hw: (b * nt + n, c))

    def cw(col0):
        return pl.BlockSpec((CONV_WIDTH, hw), lambda b, n, c=col0 // hw: (0, c))

    par_spec = pl.BlockSpec((GDN_R, LANES), lambda b, n: (0, 0))
    return pl.pallas_call(
        _gdn_kernel,
        grid=(batch, nt),
        in_specs=[tok(OFF_GQ), tok(OFF_GK), tok(OFF_GV), tok(OFF_GZ),
                  cw(0), cw(GDN_HEADS * GDN_DK), cw(2 * GDN_HEADS * GDN_DK),
                  pl.BlockSpec((GDN_TC, LANES), lambda b, n: (b * nt + n, 0)),
                  par_spec, par_spec,
                  pl.BlockSpec((1, LANES), lambda b, n: (0, 0))],
        out_specs=pl.BlockSpec((GDN_TC, hw), lambda b, n: (b * nt + n, 0)),
        out_shape=jax.ShapeDtypeStruct((batch * seq, GDN_HEADS * GDN_DV), BF16),
        scratch_shapes=[pltpu.VMEM((GDN_HB, GDN_DK, GDN_DV), F32),
                        pltpu.VMEM((3, HALO, hw), F32),
                        pltpu.VMEM((2, GDN_R, LANES), F32)],
        compiler_params=pltpu.CompilerParams(
            dimension_semantics=("arbitrary", "arbitrary"), vmem_limit_bytes=VMEM_LIMIT),
        name="gdn",
    )(main, main, main, main, conv_w, conv_w, conv_w, small, alog_rows, dtb_rows, norm_g)


DSA_TQ = 256
DSA_TK = 256
DSA_TK3 = 512
REP = DSA_HEADS // DSA_KV_HEADS


def _dsa_kernel(q_ref, qi_ref, sm_ref, k_ref, v_ref, ki_ref, z_ref, o_ref,
                keys_ref, keys16_ref, qih_ref, acc_ref, m_ref, l_ref, *, k_sel):
    i = pl.program_id(1)
    nkb = i + 1

    for hh in range(IDX_HEADS):
        qih_ref[hh] = qi_ref[:, hh * IDX_DIM:(hh + 1) * IDX_DIM]
    wt = sm_ref[...].T[SM_IW:SM_IW + IDX_HEADS, :] * (IDX_HEADS ** -0.5 * IDX_DIM ** -0.5)
    wide = DSA_TK3 // DSA_TK

    def for_key_blocks(step):
        def body(kb, carry):
            step(pl.multiple_of(kb * DSA_TK3, DSA_TK3), DSA_TK3)
            return carry

        lax.fori_loop(0, nkb // wide, body, 0)
        for rem in range(1, wide):
            @pl.when(nkb % wide >= rem)
            def _():
                step(pl.multiple_of((nkb // wide * wide + rem - 1) * DSA_TK, DSA_TK), DSA_TK)

    def sum_key_blocks(count, zero, wide_steps):
        if not wide_steps:
            return lax.fori_loop(
                0, nkb, lambda kb, c: c + count(pl.multiple_of(kb * DSA_TK, DSA_TK), DSA_TK), zero)
        cnt = lax.fori_loop(
            0, nkb // wide,
            lambda kb, c: c + count(pl.multiple_of(kb * DSA_TK3, DSA_TK3), DSA_TK3), zero)
        for rem in range(1, wide):
            off = pl.multiple_of((nkb // wide * wide + rem - 1) * DSA_TK, DSA_TK)
            cnt = lax.cond(nkb % wide >= rem, lambda c: c + count(off, DSA_TK), lambda c: c, cnt)
        return cnt

    def score_block(off, width):
        kblk = ki_ref[pl.ds(off, width), :]
        acc = jnp.zeros((width, DSA_TQ), F32)
        for hh in range(IDX_HEADS):
            p = _dot_nt(kblk, qih_ref[hh])
            acc = acc + wt[hh:hh + 1, :] * jnp.maximum(p, 0.0)
        kpos = off + lax.broadcasted_iota(I32, (width, DSA_TQ), 0)
        qpos = i * DSA_TQ + lax.broadcasted_iota(I32, (width, DSA_TQ), 1)
        sc = jnp.where(kpos <= qpos, acc, -jnp.inf)
        keys_ref[pl.ds(off, width), :] = sc
        keys16_ref[pl.ds(off, width), :] = sc.astype(BF16)

    for_key_blocks(score_block)

    def tree_sum(x, rows):
        parts = [x[r:r + rows, :] for r in range(0, x.shape[0], rows)]
        while len(parts) > 1:
            parts = [a + b for a, b in zip(parts[0::2], parts[1::2])]
        return parts[0]

    def key_to_bits(key, width):
        return key ^ ((key >> (width - 1)) & ((1 << (width - 1)) - 1))

    def p2a(p, u16):
        cand = u16 | lax.shift_left(jnp.int32(1), 15 - p)
        bits = lax.shift_left(key_to_bits(cand - 32768, 16), 16)
        c16 = pltpu.bitcast(bits, F32).astype(BF16)

        def count(off, width):
            hit = jnp.where(keys16_ref[pl.ds(off, width), :] >= c16,
                            jnp.ones((), BF16), jnp.zeros((), BF16))
            return tree_sum(hit, PACKED_ROWS).astype(F32)

        cnt = sum_key_blocks(count, jnp.zeros((PACKED_ROWS, DSA_TQ), F32), True)
        cnt = jnp.sum(cnt, axis=0, keepdims=True)
        return jnp.where(cnt >= k_sel, cand, u16)

    u16 = lax.fori_loop(0, 16, p2a, jnp.zeros((1, DSA_TQ), I32))
    coarse = lax.shift_left(key_to_bits(u16 - 32768, 16), 16)
    lo_key = key_to_bits(coarse, 32) - (1 << 15)

    def p2b(p, off_key):
        cand = off_key | lax.shift_left(jnp.int32(1), 16 - p)
        cand_f = pltpu.bitcast(key_to_bits(lo_key + cand, 32), F32)

        def count(off, width):
            hit = jnp.where(keys_ref[pl.ds(off, width), :] >= cand_f, 1, 0)
            return tree_sum(hit, SUBLANES)

        cnt = sum_key_blocks(count, jnp.zeros((SUBLANES, DSA_TQ), I32), False)
        cnt = jnp.sum(cnt, axis=0, keepdims=True)
        return jnp.where(cnt >= k_sel, cand, off_key)

    off_key = lax.fori_loop(0, 17, p2b, jnp.zeros((1, DSA_TQ), I32))
    tau = pltpu.bitcast(key_to_bits(jnp.maximum(lo_key + off_key, KEY_F32_LOWEST), 32), F32)

    m_ref[...] = jnp.full_like(m_ref, NEG_BIG)
    l_ref[...] = jnp.zeros_like(l_ref)
    acc_ref[...] = jnp.zeros_like(acc_ref)

    def attend(off, width):
        bias = jnp.where(keys_ref[pl.ds(off, width), :] >= tau, 0.0, NEG_BIG).astype(BF16)
        heads = range(DSA_HEADS)
        kblk = [k_ref[pl.ds(off, width), g * DSA_HEAD_DIM:(g + 1) * DSA_HEAD_DIM]
                for g in range(DSA_KV_HEADS)]
        vblk = [v_ref[pl.ds(off, width), g * DSA_HEAD_DIM:(g + 1) * DSA_HEAD_DIM]
                for g in range(DSA_KV_HEADS)]
        s = [_dot_nt(kblk[hh // REP],
                     q_ref[:, hh * DSA_HEAD_DIM:(hh + 1) * DSA_HEAD_DIM]).astype(BF16) + bias
             for hh in heads]
        m_old = [m_ref[hh] for hh in heads]
        m_new = [jnp.maximum(m_old[hh], jnp.max(s[hh], axis=0, keepdims=True).astype(F32))
                 for hh in heads]
        alpha = [jnp.exp2(m_old[hh] - m_new[hh]) for hh in heads]
        p = [jnp.exp2(s[hh] - m_new[hh][0:1, :].astype(BF16)) for hh in heads]
        for hh in heads:
            part = tree_sum(p[hh], PACKED_ROWS).astype(F32)
            l_ref[hh] = alpha[hh] * l_ref[hh] + jnp.sum(part, axis=0, keepdims=True)
            m_ref[hh] = m_new[hh]
        pv = [_dot_tn(vblk[hh // REP], p[hh]) for hh in heads]
        for hh in heads:
            acc_ref[hh] = alpha[hh][0:1, :] * acc_ref[hh] + pv[hh]

    for_key_blocks(attend)

    for hh in range(DSA_HEADS):
        o_t = acc_ref[hh] / l_ref[hh][0:1, :]
        z = z_ref[:, hh * DSA_HEAD_DIM:(hh + 1) * DSA_HEAD_DIM].astype(F32)
        o_ref[:, hh * DSA_HEAD_DIM:(hh + 1) * DSA_HEAD_DIM] = (
            o_t.T * (z * _sigmoid(z))).astype(BF16)


def _dsa(main, small, k_idx, batch, seq):
    nq = seq // DSA_TQ
    k_sel = min(TOPK_MAX, seq // 4)
    width = DSA_HEADS * DSA_HEAD_DIM
    kvw = DSA_KV_HEADS * DSA_HEAD_DIM

    def tokw(col0):
        return pl.BlockSpec((DSA_TQ, width), lambda b, i, c=col0 // width: (b * nq + i, c))

    return pl.pallas_call(
        functools.partial(_dsa_kernel, k_sel=k_sel),
        grid=(batch, nq),
        in_specs=[
            tokw(OFF_AQ), tokw(OFF_IQ),
            pl.BlockSpec((DSA_TQ, LANES), lambda b, i: (b * nq + i, 0)),
            pl.BlockSpec((seq, kvw), lambda b, i: (b, OFF_AK // kvw)),
            pl.BlockSpec((seq, kvw), lambda b, i: (b, OFF_AV // kvw)),
            pl.BlockSpec((seq, IDX_DIM), lambda b, i: (b, 0)),
            tokw(OFF_AZ),
        ],
        out_specs=pl.BlockSpec((DSA_TQ, width), lambda b, i: (b * nq + i, 0)),
        out_shape=jax.ShapeDtypeStruct((batch * seq, width), BF16),
        scratch_shapes=[
            pltpu.VMEM((seq, DSA_TQ), F32),
            pltpu.VMEM((seq, DSA_TQ), BF16),
            pltpu.VMEM((IDX_HEADS, DSA_TQ, IDX_DIM), BF16),
            pltpu.VMEM((DSA_HEADS, DSA_HEAD_DIM, DSA_TQ), F32),
            pltpu.VMEM((DSA_HEADS, SUBLANES, DSA_TQ), F32),
            pltpu.VMEM((DSA_HEADS, SUBLANES, DSA_TQ), F32),
        ],
        compiler_params=pltpu.CompilerParams(
            dimension_semantics=("arbitrary", "arbitrary"), vmem_limit_bytes=VMEM_LIMIT),
        name="dsa",
    )(main, main, small, main, main, k_idx, main)


OUT_TM = 512


def _out_kernel(oa_ref, ob_ref, x_ref, w_ref, g_ref, out_ref):
    wa = oa_ref.shape[1]
    mixed = _dot(oa_ref[...], w_ref[0:wa, :]) + _dot(ob_ref[...], w_ref[wa:, :])
    y = x_ref[...] + mixed
    ms = jnp.mean(y * y, axis=-1, keepdims=True)
    out_ref[...] = y * lax.rsqrt(ms + EPS) * g_ref[...]


def _output_projection(o_a, o_b, xf, w_out, g):
    n = xf.shape[0]
    wa, wb = o_a.shape[1], o_b.shape[1]
    return pl.pallas_call(
        _out_kernel,
        grid=(n // OUT_TM,),
        in_specs=[
            pl.BlockSpec((OUT_TM, wa), lambda i: (i, 0)),
            pl.BlockSpec((OUT_TM, wb), lambda i: (i, 0)),
            pl.BlockSpec((OUT_TM, D_MODEL), lambda i: (i, 0)),
            pl.BlockSpec((wa + wb, D_MODEL), lambda i: (0, 0)),
            pl.BlockSpec((1, D_MODEL), lambda i: (0, 0)),
        ],
        out_specs=pl.BlockSpec((OUT_TM, D_MODEL), lambda i: (i, 0)),
        out_shape=jax.ShapeDtypeStruct((n, D_MODEL), F32),
        compiler_params=pltpu.CompilerParams(
            dimension_semantics=("arbitrary",), vmem_limit_bytes=VMEM_LIMIT),
        name="out_proj",
    )(o_a, o_b, xf, w_out, g)


def _rope_tables(positions, head_dim):
    rot = head_dim // ROPE_FRACTION
    half = rot // 2
    lane = jnp.arange(LANES) % head_dim
    inv_freq = ROPE_THETA ** (-((lane % half).astype(F32) * 2.0 / rot))
    freq = jnp.where(lane < rot, inv_freq, 0.0)
    sign = jnp.where(lane < half, -1.0, 1.0).astype(F32)
    ang = positions.astype(F32).reshape(-1, 1) * freq
    return jnp.stack([jnp.cos(ang), jnp.sin(ang) * sign])


def _split_w_in(w):
    w_a = w[:, 0:WA_COLS].astype(BF16)
    w_b = w[:, WB_SRC0:WB_SRC0 + WB_COLS].astype(BF16)
    gab = w[:, WA_COLS:WB_SRC0]
    ikw = w[:, WB_SRC0 + WB_COLS:]
    pad = jnp.zeros((w.shape[0], LANES - gab.shape[1] - ikw.shape[1]), w.dtype)
    w_small = jnp.concatenate([ikw, gab, pad], axis=1).astype(BF16)
    return w_a, w_b, w_small


def kernel(x, positions, attn_norm_g, w_in, gdn_conv_w, gdn_a_log, gdn_dt_bias, gdn_norm_g,
           w_out, final_norm_g):
    batch, seq, d = x.shape
    assert d == D_MODEL and w_in.shape[0] == 1, "single-layer trunk with D_MODEL=2048 only"
    assert seq % max(GDN_TC, DSA_TQ) == 0 and (batch * seq) % max(PROJ_TM, OUT_TM) == 0
    xf = x.reshape(batch * seq, d)

    w_a, w_b, w_small = _split_w_in(w_in[0])
    main, small, k_idx = _input_projection(xf, attn_norm_g[0].reshape(1, d), w_a, w_b, w_small,
                                           _rope_tables(positions, DSA_HEAD_DIM),
                                           _rope_tables(positions, IDX_DIM))

    def head_rows(p):
        return jnp.broadcast_to(jnp.tile(p, GDN_NCH)[:, None], (GDN_R, LANES))

    o_a = _gdn(main, small, gdn_conv_w[0], head_rows(gdn_a_log[0]), head_rows(gdn_dt_bias[0]),
               gdn_norm_g[0].reshape(1, GDN_DV), batch, seq)
    o_b = _dsa(main, small, k_idx, batch, seq)

    out = _output_projection(o_a, o_b, xf, w_out[0].astype(BF16), final_norm_g.reshape(1, d))
    return out.reshape(batch, seq, d)
```

```python
import functools
import math

import jax
import jax.numpy as jnp
from jax import lax
from jax.experimental import pallas as pl
from jax.experimental.pallas import tpu as pltpu

F32 = jnp.float32
BF16 = jnp.bfloat16
I32 = jnp.int32

D_MODEL = 2048
GDN_HEADS = 8
GDN_DK = 128
GDN_DV = 128
CONV_WIDTH = 4
CHUNK = 64
DSA_HEADS = 8
DSA_KV_HEADS = 2
DSA_HEAD_DIM = 128
IDX_HEADS = 16
IDX_DIM = 64
TOPK_MAX = 256
ROPE_THETA = 500000.0
ROPE_FRACTION = 4
EPS = 1e-6

LANES = 128
SUBLANES = 8
PACKED_ROWS = 16
VMEM_LIMIT = 56 * 1024 * 1024

OFF_GQ, OFF_GK, OFF_GV, OFF_GZ = 0, 1024, 2048, 3072
OFF_AQ, OFF_AZ, OFF_IQ, OFF_AK, OFF_AV = 4096, 5120, 6144, 7168, 7424
MAIN_COLS = 7680
WA_COLS = 4096
WB_SRC0 = 4112
WB_COLS = 3584
WB_MAP = ((0, 1024, OFF_AQ), (1024, 256, OFF_AK), (1280, 256, OFF_AV),
          (1536, 1024, OFF_AZ), (2560, 1024, OFF_IQ))
SM_IK, SM_IW, SM_GA, SM_GB = 0, 64, 80, 88

INT_MIN = -2147483648
KEY_F32_LOWEST = INT_MIN + 0x00800000
NEG_BIG = -1e30


def _sigmoid(x):
    return 1.0 / (1.0 + jnp.exp(-x))


def _softplus(x):
    return jnp.maximum(x, 0.0) + jnp.log(1.0 + jnp.exp(-jnp.abs(x)))


def _dot(a, b):
    return jnp.dot(a, b, preferred_element_type=F32)


def _dot_nt(a, b):
    return lax.dot_general(a, b, (((1,), (1,)), ((), ())), preferred_element_type=F32)


def _dot_tn(a, b):
    return lax.dot_general(a, b, (((0,), (0,)), ((), ())), preferred_element_type=F32)


PROJ_TM = 256
PROJ_SEG = 512
HALF128 = DSA_HEAD_DIM // ROPE_FRACTION // 2
HALF64 = IDX_DIM // ROPE_FRACTION // 2
Q_SCALE = DSA_HEAD_DIM ** -0.5 * math.log2(math.e)


def _rope_slab(xs, tab, half, head_dim):
    n = xs.shape[-1]
    lane = lax.broadcasted_iota(I32, xs.shape, 1) % head_dim
    partner = jnp.where(lane < half, pltpu.roll(xs, n - half, 1), pltpu.roll(xs, half, 1))
    return xs * tab[0] + partner * tab[1]


def _split_tables(cos, sin):
    rot128, rot64 = 2 * HALF128, 2 * HALF64
    lane = lax.broadcasted_iota(I32, cos.shape, 1)

    def pick(tab, fill):
        t128 = jnp.where(lane < rot128, tab, fill)
        lo = pltpu.roll(tab, LANES - rot128, 1)
        hi = pltpu.roll(tab, IDX_DIM - rot128, 1)
        t64 = jnp.where(lane % IDX_DIM < rot64, jnp.where(lane < IDX_DIM, lo, hi), fill)
        return t128, t64

    c128, c64 = pick(cos, 1.0)
    s128, s64 = pick(sin, 0.0)
    return (c128, s128), (c64, s64)


def _col_kind(col):
    if OFF_AQ <= col < OFF_AQ + DSA_HEADS * DSA_HEAD_DIM:
        return "q"
    if OFF_AK <= col < OFF_AK + DSA_KV_HEADS * DSA_HEAD_DIM:
        return "k"
    if OFF_IQ <= col < OFF_IQ + IDX_HEADS * IDX_DIM:
        return "i"
    return "plain"


def _wb_dest(col):
    for src0, width, dst0 in WB_MAP:
        if src0 <= col < src0 + width:
            return dst0 + col - src0
    raise ValueError(col)


def _proj_kernel(x_ref, g_ref, wa_ref, wb_ref, ws_ref, cos_ref, sin_ref,
                 main_ref, small_ref, kidx_ref):
    t128, t64 = _split_tables(cos_ref[...], sin_ref[...])
    x = x_ref[...]
    ms = jnp.mean(x * x, axis=-1, keepdims=True)
    h = (x * lax.rsqrt(ms + EPS) * g_ref[...]).astype(BF16)
    s = _dot(h, ws_ref[...])
    lane = lax.broadcasted_iota(I32, s.shape, 1)
    s = jnp.where(lane < IDX_DIM, _rope_slab(s, t64, HALF64, IDX_DIM), s)
    small_ref[...] = s
    kidx_ref[...] = s[:, SM_IK:SM_IK + IDX_DIM].astype(BF16)

    def segment(w_ref, c0, dest):
        acc = _dot(h, w_ref[:, c0:c0 + PROJ_SEG])
        for c in range(0, PROJ_SEG, LANES):
            val = acc[:, c:c + LANES]
            d0 = dest(c0 + c)
            kind = _col_kind(d0)
            if kind == "q":
                val = _rope_slab(val, t128, HALF128, DSA_HEAD_DIM) * Q_SCALE
            elif kind == "k":
                val = _rope_slab(val, t128, HALF128, DSA_HEAD_DIM)
            elif kind == "i":
                val = _rope_slab(val, t64, HALF64, IDX_DIM)
            main_ref[:, d0:d0 + LANES] = val.astype(BF16)

    for c0 in range(0, WA_COLS, PROJ_SEG):
        segment(wa_ref, c0, lambda col: col)
    for c0 in range(0, WB_COLS, PROJ_SEG):
        segment(wb_ref, c0, _wb_dest)


def _input_projection(xf, g, w_a, w_b, w_small, cos, sin):
    n = xf.shape[0]
    tab_spec = pl.BlockSpec((PROJ_TM, LANES), lambda i: (i, 0))
    resident = pl.Buffered(1)
    return pl.pallas_call(
        _proj_kernel,
        grid=(n // PROJ_TM,),
        in_specs=[
            pl.BlockSpec((PROJ_TM, D_MODEL), lambda i: (i, 0)),
            pl.BlockSpec((1, D_MODEL), lambda i: (0, 0)),
            pl.BlockSpec((D_MODEL, WA_COLS), lambda i: (0, 0), pipeline_mode=resident),
            pl.BlockSpec((D_MODEL, WB_COLS), lambda i: (0, 0), pipeline_mode=resident),
            pl.BlockSpec((D_MODEL, LANES), lambda i: (0, 0), pipeline_mode=resident),
            tab_spec, tab_spec,
        ],
        out_specs=[
            pl.BlockSpec((PROJ_TM, MAIN_COLS), lambda i: (i, 0)),
            pl.BlockSpec((PROJ_TM, LANES), lambda i: (i, 0)),
            pl.BlockSpec((PROJ_TM, IDX_DIM), lambda i: (i, 0)),
        ],
        out_shape=[
            jax.ShapeDtypeStruct((n, MAIN_COLS), BF16),
            jax.ShapeDtypeStruct((n, LANES), F32),
            jax.ShapeDtypeStruct((n, IDX_DIM), BF16),
        ],
        compiler_params=pltpu.CompilerParams(
            dimension_semantics=("arbitrary",), vmem_limit_bytes=VMEM_LIMIT),
        name="in_proj",
    )(xf, g, w_a, w_b, w_small, cos, sin)


GDN_TC = 256
GDN_NCH = GDN_TC // CHUNK
GDN_HB = GDN_HEADS
GDN_R = GDN_HB * GDN_NCH
HALO = SUBLANES


def _gdn_kernel(q_ref, k_ref, v_ref, z_ref, cwq_ref, cwk_ref, cwv_ref, sm_ref,
                alog_ref, dtb_ref, ng_ref, o_ref, s_ref, carry_ref, rows_ref):
    n = pl.program_id(1)

    @pl.when(n == 0)
    def _():
        s_ref[...] = jnp.zeros_like(s_ref)
        carry_ref[...] = jnp.zeros_like(carry_ref)

    groups = GDN_TC // SUBLANES
    sub = lax.broadcasted_iota(I32, (groups, SUBLANES, GDN_HB * LANES), 1)

    def conv_silu(x_ref, w_ref, idx):
        x = x_ref[...].astype(F32)
        x3 = jnp.concatenate([carry_ref[idx], x], axis=0).reshape(groups + 1, SUBLANES, -1)
        carry_ref[idx] = x[GDN_TC - HALO:GDN_TC, :]
        w = w_ref[...]
        y = x * w[CONV_WIDTH - 1:CONV_WIDTH, :]
        for d in range(1, CONV_WIDTH):
            rot = pltpu.roll(x3, d, 1)
            shifted = jnp.where(sub < d, rot[0:groups], rot[1:groups + 1])
            y = y + shifted.reshape(GDN_TC, -1) * w[CONV_WIDTH - 1 - d:CONV_WIDTH - d, :]
        return y * _sigmoid(y)

    q = conv_silu(q_ref, cwq_ref, 0)
    k = conv_silu(k_ref, cwk_ref, 1)
    v = conv_silu(v_ref, cwv_ref, 2)

    def l2n(x):
        return x * lax.rsqrt(jnp.sum(x * x, axis=-1, keepdims=True) + EPS)

    def head(x, hl):
        return x[:, hl * LANES:(hl + 1) * LANES]

    qn = [l2n(head(q, hl)) * (GDN_DK ** -0.5) for hl in range(GDN_HB)]
    kn = [l2n(head(k, hl)) for hl in range(GDN_HB)]

    sm_t = sm_ref[...].T
    rows_ref[...] = jnp.zeros_like(rows_ref)
    for c in range(GDN_NCH):
        pos = slice(c * CHUNK, (c + 1) * CHUNK)
        rows_ref[0, c * GDN_HB:(c + 1) * GDN_HB, 0:CHUNK] = sm_t[SM_GA:SM_GA + GDN_HB, pos]
        rows_ref[1, c * GDN_HB:(c + 1) * GDN_HB, 0:CHUNK] = sm_t[SM_GB:SM_GB + GDN_HB, pos]
    g = -jnp.exp(alog_ref[...]) * _softplus(rows_ref[0] + dtb_ref[...])
    lane = lax.broadcasted_iota(I32, (GDN_R, LANES), 1)
    gc = g
    sh = 1
    while sh < CHUNK:
        gc = gc + jnp.where(lane >= sh, pltpu.roll(gc, sh, 1), 0.0)
        sh *= 2
    beta = _sigmoid(rows_ref[1])
    zpad = jnp.zeros((LANES - GDN_R, LANES), F32)
    gc_t = jnp.concatenate([gc, zpad], axis=0).T
    beta_t = jnp.concatenate([beta, zpad], axis=0).T

    ri = lax.broadcasted_iota(I32, (CHUNK, CHUNK), 0)
    ci = lax.broadcasted_iota(I32, (CHUNK, CHUNK), 1)
    incl = ri >= ci
    strict = ri > ci
    eye = jnp.where(ri == ci, 1.0, 0.0).astype(F32)
    ng = ng_ref[...]
    rs = range(GDN_R)

    qc, kc, vc, g_col, b_col, decay, eg = [], [], [], [], [], [], []
    for r in rs:
        c, hl = divmod(r, GDN_HB)
        rows = slice(c * CHUNK, (c + 1) * CHUNK)
        qc.append(qn[hl][rows])
        kc.append(kn[hl][rows])
        vc.append(head(v, hl)[rows])
        g_col.append(jnp.broadcast_to(gc_t[0:CHUNK, r:r + 1], (CHUNK, LANES)))
        b_col.append(jnp.broadcast_to(beta_t[0:CHUNK, r:r + 1], (CHUNK, LANES)))
        diff = g_col[r][:, 0:CHUNK] - gc[r:r + 1, 0:CHUNK]
        decay.append(jnp.where(incl, jnp.exp(jnp.where(incl, diff, 0.0)), 0.0))
        eg.append(jnp.exp(g_col[r]))

    qk = [_dot_nt(jnp.concatenate([qc[r], kc[r]], axis=0).astype(BF16), kc[r].astype(BF16))
          for r in rs]

    xk = [-jnp.where(strict, b_col[r][:, 0:CHUNK] * qk[r][CHUNK:] * decay[r], 0.0) for r in rs]
    tm = [eye + xk[r] for r in rs]
    xk = [_dot(xk[r].astype(BF16), xk[r].astype(BF16)) for r in rs]
    for _ in range(4):
        st = [_dot(jnp.concatenate([tm[r], xk[r]], axis=0).astype(BF16), xk[r].astype(BF16))
              for r in rs]
        tm = [tm[r] + st[r][0:CHUNK] for r in rs]
        xk = [st[r][CHUNK:] for r in rs]
    tm = [tm[r] + _dot(tm[r].astype(BF16), xk[r].astype(BF16)) for r in rs]

    wu = [_dot(tm[r].astype(BF16),
               jnp.concatenate([kc[r] * (b_col[r] * eg[r]), vc[r] * b_col[r]], axis=1).astype(BF16)
               ).astype(BF16) for r in rs]
    awu = [_dot((qk[r][0:CHUNK] * decay[r]).astype(BF16), wu[r]) for r in rs]
    g_last = [g_col[r][CHUNK - 1:CHUNK, :] for r in rs]
    mn = [_dot_tn((kc[r] * jnp.exp(g_last[r] - g_col[r])).astype(BF16), wu[r]) for r in rs]
    lhs = [jnp.concatenate([qc[r] * eg[r] - awu[r][:, 0:LANES], mn[r][:, 0:LANES]],
                           axis=0).astype(BF16) for r in rs]
    el = [jnp.exp(g_last[r]) for r in rs]

    s = [s_ref[hl] for hl in range(GDN_HB)]
    outs = {}
    for c in range(GDN_NCH):
        for hl in range(GDN_HB):
            r = c * GDN_HB + hl
            st = _dot(lhs[r], s[hl].astype(BF16))
            outs[r] = st[0:CHUNK] + awu[r][:, LANES:]
            s[hl] = s[hl] * el[r] - st[CHUNK:] + mn[r][:, LANES:]
    for hl in range(GDN_HB):
        s_ref[hl] = s[hl]
    for r in rs:
        c, hl = divmod(r, GDN_HB)
        o = outs[r]
        o = o * lax.rsqrt(jnp.mean(o * o, axis=-1, keepdims=True) + EPS) * ng
        z = z_ref[c * CHUNK:(c + 1) * CHUNK, hl * LANES:(hl + 1) * LANES].astype(F32)
        o_ref[c * CHUNK:(c + 1) * CHUNK, hl * LANES:(hl + 1) * LANES] = (
            o * (z * _sigmoid(z))).astype(BF16)


def _gdn(main, small, conv_w, alog_rows, dtb_rows, norm_g, batch, seq):
    nt = seq // GDN_TC
    hw = GDN_HB * LANES

    def tok(col0):
        return pl.BlockSpec((GDN_TC, hw), lambda b, n, c=col0 // hw: (b * nt + n, c))

    def cw(col0):
        return pl.BlockSpec((CONV_WIDTH, hw), lambda b, n, c=col0 // hw: (0, c))

    par_spec = pl.BlockSpec((GDN_R, LANES), lambda b, n: (0, 0))
    return pl.pallas_call(
        _gdn_kernel,
        grid=(batch, nt),
        in_specs=[tok(OFF_GQ), tok(OFF_GK), tok(OFF_GV), tok(OFF_GZ),
                  cw(0), cw(GDN_HEADS * GDN_DK), cw(2 * GDN_HEADS * GDN_DK),
                  pl.BlockSpec((GDN_TC, LANES), lambda b, n: (b * nt + n, 0)),
                  par_spec, par_spec,
                  pl.BlockSpec((1, LANES), lambda b, n: (0, 0))],
        out_specs=pl.BlockSpec((GDN_TC, hw), lambda b, n: (b * nt + n, 0)),
        out_shape=jax.ShapeDtypeStruct((batch * seq, GDN_HEADS * GDN_DV), BF16),
        scratch_shapes=[pltpu.VMEM((GDN_HB, GDN_DK, GDN_DV), F32),
                        pltpu.VMEM((3, HALO, hw), F32),
                        pltpu.VMEM((2, GDN_R, LANES), F32)],
        compiler_params=pltpu.CompilerParams(
            dimension_semantics=("arbitrary", "arbitrary"), vmem_limit_bytes=VMEM_LIMIT),
        name="gdn",
    )(main, main, main, main, conv_w, conv_w, conv_w, small, alog_rows, dtb_rows, norm_g)


DSA_TQ = 256
DSA_TK = 256
DSA_TK3 = 512
REP = DSA_HEADS // DSA_KV_HEADS


def _dsa_kernel(q_ref, qi_ref, sm_ref, k_ref, v_ref, ki_ref, z_ref, o_ref,
                keys_ref, keys16_ref, qih_ref, acc_ref, m_ref, l_ref, *, k_sel):
    i = pl.program_id(1)
    nkb = i + 1

    for hh in range(IDX_HEADS):
        qih_ref[hh] = qi_ref[:, hh * IDX_DIM:(hh + 1) * IDX_DIM]
    wt = sm_ref[...].T[SM_IW:SM_IW + IDX_HEADS, :] * (IDX_HEADS ** -0.5 * IDX_DIM ** -0.5)
    wide = DSA_TK3 // DSA_TK

    def for_key_blocks(step):
        def body(kb, carry):
            step(pl.multiple_of(kb * DSA_TK3, DSA_TK3), DSA_TK3)
            return carry

        lax.fori_loop(0, nkb // wide, body, 0)
        for rem in range(1, wide):
            @pl.when(nkb % wide >= rem)
            def _():
                step(pl.multiple_of((nkb // wide * wide + rem - 1) * DSA_TK, DSA_TK), DSA_TK)

    def sum_key_blocks(count, zero, wide_steps):
        if not wide_steps:
            return lax.fori_loop(
                0, nkb, lambda kb, c: c + count(pl.multiple_of(kb * DSA_TK, DSA_TK), DSA_TK), zero)
        cnt = lax.fori_loop(
            0, nkb // wide,
            lambda kb, c: c + count(pl.multiple_of(kb * DSA_TK3, DSA_TK3), DSA_TK3), zero)
        for rem in range(1, wide):
            off = pl.multiple_of((nkb // wide * wide + rem - 1) * DSA_TK, DSA_TK)
            cnt = lax.cond(nkb % wide >= rem, lambda c: c + count(off, DSA_TK), lambda c: c, cnt)
        return cnt

    def score_block(off, width):
        kblk = ki_ref[pl.ds(off, width), :]
        acc = jnp.zeros((width, DSA_TQ), F32)
        for hh in range(IDX_HEADS):
            p = _dot_nt(kblk, qih_ref[hh])
            acc = acc + wt[hh:hh + 1, :] * jnp.maximum(p, 0.0)
        kpos = off + lax.broadcasted_iota(I32, (width, DSA_TQ), 0)
        qpos = i * DSA_TQ + lax.broadcasted_iota(I32, (width, DSA_TQ), 1)
        sc = jnp.where(kpos <= qpos, acc, -jnp.inf)
        keys_ref[pl.ds(off, width), :] = sc
        keys16_ref[pl.ds(off, width), :] = sc.astype(BF16)

    for_key_blocks(score_block)

    def tree_sum(x, rows):
        parts = [x[r:r + rows, :] for r in range(0, x.shape[0], rows)]
        while len(parts) > 1:
            parts = [a + b for a, b in zip(parts[0::2], parts[1::2])]
        return parts[0]

    def key_to_bits(key, width):
        return key ^ ((key >> (width - 1)) & ((1 << (width - 1)) - 1))

    def p2a(p, u16):
        cand = u16 | lax.shift_left(jnp.int32(1), 15 - p)
        bits = lax.shift_left(key_to_bits(cand - 32768, 16), 16)
        c16 = pltpu.bitcast(bits, F32).astype(BF16)

        def count(off, width):
            hit = jnp.where(keys16_ref[pl.ds(off, width), :] >= c16,
                            jnp.ones((), BF16), jnp.zeros((), BF16))
            return tree_sum(hit, PACKED_ROWS).astype(F32)

        cnt = sum_key_blocks(count, jnp.zeros((PACKED_ROWS, DSA_TQ), F32), True)
        cnt = jnp.sum(cnt, axis=0, keepdims=True)
        return jnp.where(cnt >= k_sel, cand, u16)

    u16 = lax.fori_loop(0, 16, p2a, jnp.zeros((1, DSA_TQ), I32))
    coarse = lax.shift_left(key_to_bits(u16 - 32768, 16), 16)
    lo_key = key_to_bits(coarse, 32) - (1 << 15)

    def p2b(p, off_key):
        cand = off_key | lax.shift_left(jnp.int32(1), 16 - p)
        cand_f = pltpu.bitcast(key_to_bits(lo_key + cand, 32), F32)

        def count(off, width):
            hit = jnp.where(keys_ref[pl.ds(off, width), :] >= cand_f, 1, 0)
            return tree_sum(hit, SUBLANES)

        cnt = sum_key_blocks(count, jnp.zeros((SUBLANES, DSA_TQ), I32), False)
        cnt = jnp.sum(cnt, axis=0, keepdims=True)
        return jnp.where(cnt >= k_sel, cand, off_key)

    off_key = lax.fori_loop(0, 17, p2b, jnp.zeros((1, DSA_TQ), I32))
    tau = pltpu.bitcast(key_to_bits(jnp.maximum(lo_key + off_key, KEY_F32_LOWEST), 32), F32)

    m_ref[...] = jnp.full_like(m_ref, NEG_BIG)
    l_ref[...] = jnp.zeros_like(l_ref)
    acc_ref[...] = jnp.zeros_like(acc_ref)

    def attend(off, width):
        bias = jnp.where(keys_ref[pl.ds(off, width), :] >= tau, 0.0, NEG_BIG).astype(BF16)
        heads = range(DSA_HEADS)
        kblk = [k_ref[pl.ds(off, width), g * DSA_HEAD_DIM:(g + 1) * DSA_HEAD_DIM]
                for g in range(DSA_KV_HEADS)]
        vblk = [v_ref[pl.ds(off, width), g * DSA_HEAD_DIM:(g + 1) * DSA_HEAD_DIM]
                for g in range(DSA_KV_HEADS)]
        s = [_dot_nt(kblk[hh // REP],
                     q_ref[:, hh * DSA_HEAD_DIM:(hh + 1) * DSA_HEAD_DIM]).astype(BF16) + bias
             for hh in heads]
        m_old = [m_ref[hh] for hh in heads]
        m_new = [jnp.maximum(m_old[hh], jnp.max(s[hh], axis=0, keepdims=True).astype(F32))
                 for hh in heads]
        alpha = [jnp.exp2(m_old[hh] - m_new[hh]) for hh in heads]
        p = [jnp.exp2(s[hh] - m_new[hh][0:1, :].astype(BF16)) for hh in heads]
        for hh in heads:
            part = tree_sum(p[hh], PACKED_ROWS).astype(F32)
            l_ref[hh] = alpha[hh] * l_ref[hh] + jnp.sum(part, axis=0, keepdims=True)
            m_ref[hh] = m_new[hh]
        pv = [_dot_tn(vblk[hh // REP], p[hh]) for hh in heads]
        for hh in heads:
            acc_ref[hh] = alpha[hh][0:1, :] * acc_ref[hh] + pv[hh]

    for_key_blocks(attend)

    for hh in range(DSA_HEADS):
        o_t = acc_ref[hh] / l_ref[hh][0:1, :]
        z = z_ref[:, hh * DSA_HEAD_DIM:(hh + 1) * DSA_HEAD_DIM].astype(F32)
        o_ref[:, hh * DSA_HEAD_DIM:(hh + 1) * DSA_HEAD_DIM] = (
            o_t.T * (z * _sigmoid(z))).astype(BF16)


def _dsa(main, small, k_idx, batch, seq):
    nq = seq // DSA_TQ
    k_sel = min(TOPK_MAX, seq // 4)
    width = DSA_HEADS * DSA_HEAD_DIM
    kvw = DSA_KV_HEADS * DSA_HEAD_DIM

    def tokw(col0):
        return pl.BlockSpec((DSA_TQ, width), lambda b, i, c=col0 // width: (b * nq + i, c))

    return pl.pallas_call(
        functools.partial(_dsa_kernel, k_sel=k_sel),
        grid=(batch, nq),
        in_specs=[
            tokw(OFF_AQ), tokw(OFF_IQ),
            pl.BlockSpec((DSA_TQ, LANES), lambda b, i: (b * nq + i, 0)),
            pl.BlockSpec((seq, kvw), lambda b, i: (b, OFF_AK // kvw)),
            pl.BlockSpec((seq, kvw), lambda b, i: (b, OFF_AV // kvw)),
            pl.BlockSpec((seq, IDX_DIM), lambda b, i: (b, 0)),
            tokw(OFF_AZ),
        ],
        out_specs=pl.BlockSpec((DSA_TQ, width), lambda b, i: (b * nq + i, 0)),
        out_shape=jax.ShapeDtypeStruct((batch * seq, width), BF16),
        scratch_shapes=[
            pltpu.VMEM((seq, DSA_TQ), F32),
            pltpu.VMEM((seq, DSA_TQ), BF16),
            pltpu.VMEM((IDX_HEADS, DSA_TQ, IDX_DIM), BF16),
            pltpu.VMEM((DSA_HEADS, DSA_HEAD_DIM, DSA_TQ), F32),
            pltpu.VMEM((DSA_HEADS, SUBLANES, DSA_TQ), F32),
            pltpu.VMEM((DSA_HEADS, SUBLANES, DSA_TQ), F32),
        ],
        compiler_params=pltpu.CompilerParams(
            dimension_semantics=("arbitrary", "arbitrary"), vmem_limit_bytes=VMEM_LIMIT),
        name="dsa",
    )(main, main, small, main, main, k_idx, main)


OUT_TM = 512


def _out_kernel(oa_ref, ob_ref, x_ref, w_ref, g_ref, out_ref):
    wa = oa_ref.shape[1]
    mixed = _dot(oa_ref[...], w_ref[0:wa, :]) + _dot(ob_ref[...], w_ref[wa:, :])
    y = x_ref[...] + mixed
    ms = jnp.mean(y * y, axis=-1, keepdims=True)
    out_ref[...] = y * lax.rsqrt(ms + EPS) * g_ref[...]


def _output_projection(o_a, o_b, xf, w_out, g):
    n = xf.shape[0]
    wa, wb = o_a.shape[1], o_b.shape[1]
    return pl.pallas_call(
        _out_kernel,
        grid=(n // OUT_TM,),
        in_specs=[
            pl.BlockSpec((OUT_TM, wa), lambda i: (i, 0)),
            pl.BlockSpec((OUT_TM, wb), lambda i: (i, 0)),
            pl.BlockSpec((OUT_TM, D_MODEL), lambda i: (i, 0)),
            pl.BlockSpec((wa + wb, D_MODEL), lambda i: (0, 0)),
            pl.BlockSpec((1, D_MODEL), lambda i: (0, 0)),
        ],
        out_specs=pl.BlockSpec((OUT_TM, D_MODEL), lambda i: (i, 0)),
        out_shape=jax.ShapeDtypeStruct((n, D_MODEL), F32),
        compiler_params=pltpu.CompilerParams(
            dimension_semantics=("arbitrary",), vmem_limit_bytes=VMEM_LIMIT),
        name="out_proj",
    )(o_a, o_b, xf, w_out, g)


def _rope_table(positions):
    lane = jnp.arange(LANES)
    rot128, rot64 = 2 * HALF128, 2 * HALF64
    j128 = (lane % HALF128).astype(F32) * 2.0 / rot128
    j64 = (lane % HALF64).astype(F32) * 2.0 / rot64
    freq = jnp.where(lane < rot128, ROPE_THETA ** (-j128),
                     jnp.where(lane < rot128 + rot64, ROPE_THETA ** (-j64), 0.0))
    first = jnp.where(lane < rot128, lane < HALF128, (lane - rot128) < HALF64)
    sign = jnp.where(first, -1.0, 1.0).astype(F32)
    ang = positions.astype(F32).reshape(-1, 1) * freq
    return jnp.cos(ang), jnp.sin(ang) * sign


def _split_w_in(w):
    w_a = w[:, 0:WA_COLS].astype(BF16)
    w_b = w[:, WB_SRC0:WB_SRC0 + WB_COLS].astype(BF16)
    gab = w[:, WA_COLS:WB_SRC0]
    ikw = w[:, WB_SRC0 + WB_COLS:]
    pad = jnp.zeros((w.shape[0], LANES - gab.shape[1] - ikw.shape[1]), w.dtype)
    w_small = jnp.concatenate([ikw, gab, pad], axis=1).astype(BF16)
    return w_a, w_b, w_small


def kernel(x, positions, attn_norm_g, w_in, gdn_conv_w, gdn_a_log, gdn_dt_bias, gdn_norm_g,
           w_out, final_norm_g):
    batch, seq, d = x.shape
    assert d == D_MODEL and w_in.shape[0] == 1, "single-layer trunk with D_MODEL=2048 only"
    assert seq % max(GDN_TC, DSA_TQ) == 0 and (batch * seq) % max(PROJ_TM, OUT_TM) == 0
    xf = x.reshape(batch * seq, d)

    w_a, w_b, w_small = _split_w_in(w_in[0])
    main, small, k_idx = _input_projection(xf, attn_norm_g[0].reshape(1, d), w_a, w_b, w_small,
                                           *_rope_table(positions))

    def head_rows(p):
        return jnp.broadcast_to(jnp.tile(p, GDN_NCH)[:, None], (GDN_R, LANES))

    o_a = _gdn(main, small, gdn_conv_w[0], head_rows(gdn_a_log[0]), head_rows(gdn_dt_bias[0]),
               gdn_norm_g[0].reshape(1, GDN_DV), batch, seq)
    o_b = _dsa(main, small, k_idx, batch, seq)

    out = _output_projection(o_a, o_b, xf, w_out[0].astype(BF16), final_norm_g.reshape(1, d))
    return out.reshape(batch, seq, d)
```

```python
import functools
import math

import jax
import jax.numpy as jnp
from jax import lax
from jax.experimental import pallas as pl
from jax.experimental.pallas import tpu as pltpu

F32 = jnp.float32
BF16 = jnp.bfloat16
I32 = jnp.int32

D_MODEL = 2048
GDN_HEADS = 8
GDN_DK = 128
GDN_DV = 128
CONV_WIDTH = 4
CHUNK = 64
DSA_HEADS = 8
DSA_KV_HEADS = 2
DSA_HEAD_DIM = 128
IDX_HEADS = 16
IDX_DIM = 64
TOPK_MAX = 256
ROPE_THETA = 500000.0
ROPE_FRACTION = 4
EPS = 1e-6

LANES = 128
SUBLANES = 8
PACKED_ROWS = 16
VMEM_LIMIT = 56 * 1024 * 1024

OFF_GQ, OFF_GK, OFF_GV, OFF_GZ = 0, 1024, 2048, 3072
OFF_AQ, OFF_AZ, OFF_IQ, OFF_AK, OFF_AV = 4096, 5120, 6144, 7168, 7424
MAIN_COLS = 7680
WA_COLS = 4096
WB_SRC0 = 4112
WB_COLS = 3584
WB_MAP = ((0, 1024, OFF_AQ), (1024, 256, OFF_AK), (1280, 256, OFF_AV),
          (1536, 1024, OFF_AZ), (2560, 1024, OFF_IQ))
SM_IK, SM_IW, SM_GA, SM_GB = 0, 64, 80, 88

INT_MIN = -2147483648
KEY_F32_LOWEST = INT_MIN + 0x00800000
NEG_BIG = -1e30


def _sigmoid(x):
    return 1.0 / (1.0 + jnp.exp(-x))


def _softplus(x):
    return jnp.maximum(x, 0.0) + jnp.log(1.0 + jnp.exp(-jnp.abs(x)))


def _dot(a, b):
    return jnp.dot(a, b, preferred_element_type=F32)


def _dot_nt(a, b):
    return lax.dot_general(a, b, (((1,), (1,)), ((), ())), preferred_element_type=F32)


def _dot_tn(a, b):
    return lax.dot_general(a, b, (((0,), (0,)), ((), ())), preferred_element_type=F32)


PROJ_TM = 256
PROJ_SEG = 512
HALF128 = DSA_HEAD_DIM // ROPE_FRACTION // 2
HALF64 = IDX_DIM // ROPE_FRACTION // 2
Q_SCALE = DSA_HEAD_DIM ** -0.5 * math.log2(math.e)


def _rope_slab(xs, tab, half, head_dim):
    n = xs.shape[-1]
    lane = lax.broadcasted_iota(I32, xs.shape, 1) % head_dim
    partner = jnp.where(lane < half, pltpu.roll(xs, n - half, 1), pltpu.roll(xs, half, 1))
    return xs * tab[0] + partner * tab[1]


def _split_tables(cos, sin):
    rot128, rot64 = 2 * HALF128, 2 * HALF64
    lane = lax.broadcasted_iota(I32, cos.shape, 1)

    def pick(tab, fill):
        t128 = jnp.where(lane < rot128, tab, fill)
        lo = pltpu.roll(tab, LANES - rot128, 1)
        hi = pltpu.roll(tab, IDX_DIM - rot128, 1)
        t64 = jnp.where(lane % IDX_DIM < rot64, jnp.where(lane < IDX_DIM, lo, hi), fill)
        return t128, t64

    c128, c64 = pick(cos, 1.0)
    s128, s64 = pick(sin, 0.0)
    return (c128, s128), (c64, s64)


def _col_kind(col):
    if OFF_AQ <= col < OFF_AQ + DSA_HEADS * DSA_HEAD_DIM:
        return "q"
    if OFF_AK <= col < OFF_AK + DSA_KV_HEADS * DSA_HEAD_DIM:
        return "k"
    if OFF_IQ <= col < OFF_IQ + IDX_HEADS * IDX_DIM:
        return "i"
    return "plain"


def _wb_dest(col):
    for src0, width, dst0 in WB_MAP:
        if src0 <= col < src0 + width:
            return dst0 + col - src0
    raise ValueError(col)


def _proj_kernel(x_ref, g_ref, wa_ref, wb_ref, ws_ref, cos_ref, sin_ref,
                 main_ref, small_ref, kidx_ref):
    t128, t64 = _split_tables(cos_ref[...], sin_ref[...])
    x = x_ref[...]
    ms = jnp.mean(x * x, axis=-1, keepdims=True)
    h = (x * lax.rsqrt(ms + EPS) * g_ref[...]).astype(BF16)
    s = _dot(h, ws_ref[...])
    lane = lax.broadcasted_iota(I32, s.shape, 1)
    s = jnp.where(lane < IDX_DIM, _rope_slab(s, t64, HALF64, IDX_DIM), s)
    small_ref[...] = s
    kidx_ref[...] = s[:, SM_IK:SM_IK + IDX_DIM].astype(BF16)

    def segment(w_ref, c0, dest):
        acc = _dot(h, w_ref[:, c0:c0 + PROJ_SEG])
        for c in range(0, PROJ_SEG, LANES):
            val = acc[:, c:c + LANES]
            d0 = dest(c0 + c)
            kind = _col_kind(d0)
            if kind == "q":
                val = _rope_slab(val, t128, HALF128, DSA_HEAD_DIM) * Q_SCALE
            elif kind == "k":
                val = _rope_slab(val, t128, HALF128, DSA_HEAD_DIM)
            elif kind == "i":
                val = _rope_slab(val, t64, HALF64, IDX_DIM)
            main_ref[:, d0:d0 + LANES] = val.astype(BF16)

    for c0 in range(0, WA_COLS, PROJ_SEG):
        segment(wa_ref, c0, lambda col: col)
    for c0 in range(0, WB_COLS, PROJ_SEG):
        segment(wb_ref, c0, _wb_dest)


def _input_projection(xf, g, w_a, w_b, w_small, cos, sin):
    n = xf.shape[0]
    tab_spec = pl.BlockSpec((PROJ_TM, LANES), lambda i: (i, 0))
    resident = pl.Buffered(1)
    return pl.pallas_call(
        _proj_kernel,
        grid=(n // PROJ_TM,),
        in_specs=[
            pl.BlockSpec((PROJ_TM, D_MODEL), lambda i: (i, 0)),
            pl.BlockSpec((1, D_MODEL), lambda i: (0, 0)),
            pl.BlockSpec((D_MODEL, WA_COLS), lambda i: (0, 0), pipeline_mode=resident),
            pl.BlockSpec((D_MODEL, WB_COLS), lambda i: (0, 0), pipeline_mode=resident),
            pl.BlockSpec((D_MODEL, LANES), lambda i: (0, 0), pipeline_mode=resident),
            tab_spec, tab_spec,
        ],
        out_specs=[
            pl.BlockSpec((PROJ_TM, MAIN_COLS), lambda i: (i, 0)),
            pl.BlockSpec((PROJ_TM, LANES), lambda i: (i, 0)),
            pl.BlockSpec((PROJ_TM, IDX_DIM), lambda i: (i, 0)),
        ],
        out_shape=[
            jax.ShapeDtypeStruct((n, MAIN_COLS), BF16),
            jax.ShapeDtypeStruct((n, LANES), F32),
            jax.ShapeDtypeStruct((n, IDX_DIM), BF16),
        ],
        compiler_params=pltpu.CompilerParams(
            dimension_semantics=("arbitrary",), vmem_limit_bytes=VMEM_LIMIT),
        name="in_proj",
    )(xf, g, w_a, w_b, w_small, cos, sin)


GDN_TC = 256
GDN_NCH = GDN_TC // CHUNK
GDN_HB = GDN_HEADS
GDN_R = GDN_HB * GDN_NCH
HALO = SUBLANES


def _gdn_kernel(q_ref, k_ref, v_ref, z_ref, cwq_ref, cwk_ref, cwv_ref, sm_ref,
                alog_ref, dtb_ref, ng_ref, o_ref, s_ref, carry_ref, rows_ref):
    n = pl.program_id(1)

    @pl.when(n == 0)
    def _():
        s_ref[...] = jnp.zeros_like(s_ref)
        carry_ref[...] = jnp.zeros_like(carry_ref)

    groups = GDN_TC // SUBLANES
    sub = lax.broadcasted_iota(I32, (groups, SUBLANES, GDN_HB * LANES), 1)

    def conv_silu(x_ref, w_ref, idx):
        x = x_ref[...].astype(F32)
        x3 = jnp.concatenate([carry_ref[idx], x], axis=0).reshape(groups + 1, SUBLANES, -1)
        carry_ref[idx] = x[GDN_TC - HALO:GDN_TC, :]
        w = w_ref[...]
        y = x * w[CONV_WIDTH - 1:CONV_WIDTH, :]
        for d in range(1, CONV_WIDTH):
            rot = pltpu.roll(x3, d, 1)
            shifted = jnp.where(sub < d, rot[0:groups], rot[1:groups + 1])
            y = y + shifted.reshape(GDN_TC, -1) * w[CONV_WIDTH - 1 - d:CONV_WIDTH - d, :]
        return y * _sigmoid(y)

    q = conv_silu(q_ref, cwq_ref, 0)
    k = conv_silu(k_ref, cwk_ref, 1)
    v = conv_silu(v_ref, cwv_ref, 2)

    def l2n(x):
        return x * lax.rsqrt(jnp.sum(x * x, axis=-1, keepdims=True) + EPS)

    def head(x, hl):
        return x[:, hl * LANES:(hl + 1) * LANES]

    qn = [l2n(head(q, hl)) * (GDN_DK ** -0.5) for hl in range(GDN_HB)]
    kn = [l2n(head(k, hl)) for hl in range(GDN_HB)]

    sm_t = sm_ref[...].T
    rows_ref[...] = jnp.zeros_like(rows_ref)
    for c in range(GDN_NCH):
        pos = slice(c * CHUNK, (c + 1) * CHUNK)
        rows_ref[0, c * GDN_HB:(c + 1) * GDN_HB, 0:CHUNK] = sm_t[SM_GA:SM_GA + GDN_HB, pos]
        rows_ref[1, c * GDN_HB:(c + 1) * GDN_HB, 0:CHUNK] = sm_t[SM_GB:SM_GB + GDN_HB, pos]
    g = -jnp.exp(alog_ref[...]) * _softplus(rows_ref[0] + dtb_ref[...])
    lane = lax.broadcasted_iota(I32, (GDN_R, LANES), 1)
    gc = g
    sh = 1
    while sh < CHUNK:
        gc = gc + jnp.where(lane >= sh, pltpu.roll(gc, sh, 1), 0.0)
        sh *= 2
    beta = _sigmoid(rows_ref[1])
    zpad = jnp.zeros((LANES - GDN_R, LANES), F32)
    gc_t = jnp.concatenate([gc, zpad], axis=0).T
    beta_t = jnp.concatenate([beta, zpad], axis=0).T

    ri = lax.broadcasted_iota(I32, (CHUNK, CHUNK), 0)
    ci = lax.broadcasted_iota(I32, (CHUNK, CHUNK), 1)
    incl = ri >= ci
    strict = ri > ci
    eye = jnp.where(ri == ci, 1.0, 0.0).astype(F32)
    ng = ng_ref[...]
    rs = range(GDN_R)

    qc, kc, vc, g_col, b_col, decay, eg = [], [], [], [], [], [], []
    for r in rs:
        c, hl = divmod(r, GDN_HB)
        rows = slice(c * CHUNK, (c + 1) * CHUNK)
        qc.append(qn[hl][rows])
        kc.append(kn[hl][rows])
        vc.append(head(v, hl)[rows])
        g_col.append(jnp.broadcast_to(gc_t[0:CHUNK, r:r + 1], (CHUNK, LANES)))
        b_col.append(jnp.broadcast_to(beta_t[0:CHUNK, r:r + 1], (CHUNK, LANES)))
        diff = g_col[r][:, 0:CHUNK] - gc[r:r + 1, 0:CHUNK]
        decay.append(jnp.where(incl, jnp.exp(jnp.where(incl, diff, 0.0)), 0.0))
        eg.append(jnp.exp(g_col[r]))

    qk = [_dot_nt(jnp.concatenate([qc[r], kc[r]], axis=0).astype(BF16), kc[r].astype(BF16))
          for r in rs]

    xk = [-jnp.where(strict, b_col[r][:, 0:CHUNK] * qk[r][CHUNK:] * decay[r], 0.0) for r in rs]
    tm = [eye + xk[r] for r in rs]
    xk = [_dot(xk[r].astype(BF16), xk[r].astype(BF16)) for r in rs]
    for _ in range(4):
        st = [_dot(jnp.concatenate([tm[r], xk[r]], axis=0).astype(BF16), xk[r].astype(BF16))
              for r in rs]
        tm = [tm[r] + st[r][0:CHUNK] for r in rs]
        xk = [st[r][CHUNK:] for r in rs]
    tm = [tm[r] + _dot(tm[r].astype(BF16), xk[r].astype(BF16)) for r in rs]

    wu = [_dot(tm[r].astype(BF16),
               jnp.concatenate([kc[r] * (b_col[r] * eg[r]), vc[r] * b_col[r]], axis=1).astype(BF16)
               ).astype(BF16) for r in rs]
    awu = [_dot((qk[r][0:CHUNK] * decay[r]).astype(BF16), wu[r]) for r in rs]
    g_last = [g_col[r][CHUNK - 1:CHUNK, :] for r in rs]
    mn = [_dot_tn((kc[r] * jnp.exp(g_last[r] - g_col[r])).astype(BF16), wu[r]) for r in rs]
    lhs = [jnp.concatenate([qc[r] * eg[r] - awu[r][:, 0:LANES], mn[r][:, 0:LANES]],
                           axis=0).astype(BF16) for r in rs]
    el = [jnp.exp(g_last[r]) for r in rs]

    s = [s_ref[hl] for hl in range(GDN_HB)]
    outs = {}
    for c in range(GDN_NCH):
        for hl in range(GDN_HB):
            r = c * GDN_HB + hl
            st = _dot(lhs[r], s[hl].astype(BF16))
            outs[r] = st[0:CHUNK] + awu[r][:, LANES:]
            s[hl] = s[hl] * el[r] - st[CHUNK:] + mn[r][:, LANES:]
    for hl in range(GDN_HB):
        s_ref[hl] = s[hl]
    for r in rs:
        c, hl = divmod(r, GDN_HB)
        o = outs[r]
        o = o * lax.rsqrt(jnp.mean(o * o, axis=-1, keepdims=True) + EPS) * ng
        z = z_ref[c * CHUNK:(c + 1) * CHUNK, hl * LANES:(hl + 1) * LANES].astype(F32)
        o_ref[c * CHUNK:(c + 1) * CHUNK, hl * LANES:(hl + 1) * LANES] = (
            o * (z * _sigmoid(z))).astype(BF16)


def _gdn(main, small, conv_w, alog_rows, dtb_rows, norm_g, batch, seq):
    nt = seq // GDN_TC
    hw = GDN_HB * LANES

    def tok(col0):
        return pl.BlockSpec((GDN_TC, hw), lambda b, n, c=col0 // hw: (b * nt + n, c))

    def cw(col0):
        return pl.BlockSpec((CONV_WIDTH, hw), lambda b, n, c=col0 // hw: (0, c))

    par_spec = pl.BlockSpec((GDN_R, LANES), lambda b, n: (0, 0))
    return pl.pallas_call(
        _gdn_kernel,
        grid=(batch, nt),
        in_specs=[tok(OFF_GQ), tok(OFF_GK), tok(OFF_GV), tok(OFF_GZ),
                  cw(0), cw(GDN_HEADS * GDN_DK), cw(2 * GDN_HEADS * GDN_DK),
                  pl.BlockSpec((GDN_TC, LANES), lambda b, n: (b * nt + n, 0)),
                  par_spec, par_spec,
                  pl.BlockSpec((1, LANES), lambda b, n: (0, 0))],
        out_specs=pl.BlockSpec((GDN_TC, hw), lambda b, n: (b * nt + n, 0)),
        out_shape=jax.ShapeDtypeStruct((batch * seq, GDN_HEADS * GDN_DV), BF16),
        scratch_shapes=[pltpu.VMEM((GDN_HB, GDN_DK, GDN_DV), F32),
                        pltpu.VMEM((3, HALO, hw), F32),
                        pltpu.VMEM((2, GDN_R, LANES), F32)],
        compiler_params=pltpu.CompilerParams(
            dimension_semantics=("arbitrary", "arbitrary"), vmem_limit_bytes=VMEM_LIMIT),
        name="gdn",
    )(main, main, main, main, conv_w, conv_w, conv_w, small, alog_rows, dtb_rows, norm_g)


DSA_TQ = 256
DSA_TK = 256
DSA_TK3 = 512
REP = DSA_HEADS // DSA_KV_HEADS


def _dsa_kernel(q_ref, qi_ref, sm_ref, k_ref, v_ref, ki_ref, z_ref, o_ref,
                keys_ref, keys16_ref, qih_ref, acc_ref, m_ref, l_ref, *, k_sel):
    i = pl.program_id(1)
    nkb = i + 1

    for hh in range(IDX_HEADS):
        qih_ref[hh] = qi_ref[:, hh * IDX_DIM:(hh + 1) * IDX_DIM]
    wt = sm_ref[...].T[SM_IW:SM_IW + IDX_HEADS, :] * (IDX_HEADS ** -0.5 * IDX_DIM ** -0.5)
    wide = DSA_TK3 // DSA_TK

    def for_key_blocks(step):
        def body(kb, carry):
            step(pl.multiple_of(kb * DSA_TK3, DSA_TK3), DSA_TK3)
            return carry

        lax.fori_loop(0, nkb // wide, body, 0)
        for rem in range(1, wide):
            @pl.when(nkb % wide >= rem)
            def _():
                step(pl.multiple_of((nkb // wide * wide + rem - 1) * DSA_TK, DSA_TK), DSA_TK)

    def sum_key_blocks(count, zero, wide_steps):
        if not wide_steps:
            return lax.fori_loop(
                0, nkb, lambda kb, c: c + count(pl.multiple_of(kb * DSA_TK, DSA_TK), DSA_TK), zero)
        cnt = lax.fori_loop(
            0, nkb // wide,
            lambda kb, c: c + count(pl.multiple_of(kb * DSA_TK3, DSA_TK3), DSA_TK3), zero)
        for rem in range(1, wide):
            off = pl.multiple_of((nkb // wide * wide + rem - 1) * DSA_TK, DSA_TK)
            cnt = lax.cond(nkb % wide >= rem, lambda c: c + count(off, DSA_TK), lambda c: c, cnt)
        return cnt

    def score_block(off, width):
        kblk = ki_ref[pl.ds(off, width), :]
        acc = jnp.zeros((width, DSA_TQ), F32)
        for hh in range(IDX_HEADS):
            p = _dot_nt(kblk, qih_ref[hh])
            acc = acc + wt[hh:hh + 1, :] * jnp.maximum(p, 0.0)
        kpos = off + lax.broadcasted_iota(I32, (width, DSA_TQ), 0)
        qpos = i * DSA_TQ + lax.broadcasted_iota(I32, (width, DSA_TQ), 1)
        sc = jnp.where(kpos <= qpos, acc, -jnp.inf)
        keys_ref[pl.ds(off, width), :] = sc
        keys16_ref[pl.ds(off, width), :] = sc.astype(BF16)

    for_key_blocks(score_block)

    def tree_sum(x, rows):
        parts = [x[r:r + rows, :] for r in range(0, x.shape[0], rows)]
        while len(parts) > 1:
            parts = [a + b for a, b in zip(parts[0::2], parts[1::2])]
        return parts[0]

    def key_to_bits(key, width):
        return key ^ ((key >> (width - 1)) & ((1 << (width - 1)) - 1))

    def p2a(p, u16):
        cand = u16 | lax.shift_left(jnp.int32(1), 15 - p)
        bits = lax.shift_left(key_to_bits(cand - 32768, 16), 16)
        c16 = pltpu.bitcast(bits, F32).astype(BF16)

        def count(off, width):
            hit = jnp.where(keys16_ref[pl.ds(off, width), :] >= c16,
                            jnp.ones((), BF16), jnp.zeros((), BF16))
            return tree_sum(hit, PACKED_ROWS).astype(F32)

        cnt = sum_key_blocks(count, jnp.zeros((PACKED_ROWS, DSA_TQ), F32), True)
        cnt = jnp.sum(cnt, axis=0, keepdims=True)
        return jnp.where(cnt >= k_sel, cand, u16)

    u16 = lax.fori_loop(0, 16, p2a, jnp.zeros((1, DSA_TQ), I32))
    coarse = lax.shift_left(key_to_bits(u16 - 32768, 16), 16)
    lo_key = key_to_bits(coarse, 32) - (1 << 15)

    def p2b(p, off_key):
        cand = off_key | lax.shift_left(jnp.int32(1), 16 - p)
        cand_f = pltpu.bitcast(key_to_bits(lo_key + cand, 32), F32)

        def count(off, width):
            hit = jnp.where(keys_ref[pl.ds(off, width), :] >= cand_f, 1, 0)
            return tree_sum(hit, SUBLANES)

        cnt = sum_key_blocks(count, jnp.zeros((SUBLANES, DSA_TQ), I32), False)
        cnt = jnp.sum(cnt, axis=0, keepdims=True)
        return jnp.where(cnt >= k_sel, cand, off_key)

    off_key = lax.fori_loop(0, 17, p2b, jnp.zeros((1, DSA_TQ), I32))
    tau = pltpu.bitcast(key_to_bits(jnp.maximum(lo_key + off_key, KEY_F32_LOWEST), 32), F32)

    m_ref[...] = jnp.full_like(m_ref, NEG_BIG)
    l_ref[...] = jnp.zeros_like(l_ref)
    acc_ref[...] = jnp.zeros_like(acc_ref)

    def attend(off, width):
        bias = jnp.where(keys_ref[pl.ds(off, width), :] >= tau, 0.0, NEG_BIG).astype(BF16)
        heads = range(DSA_HEADS)
        kblk = [k_ref[pl.ds(off, width), g * DSA_HEAD_DIM:(g + 1) * DSA_HEAD_DIM]
                for g in range(DSA_KV_HEADS)]
        vblk = [v_ref[pl.ds(off, width), g * DSA_HEAD_DIM:(g + 1) * DSA_HEAD_DIM]
                for g in range(DSA_KV_HEADS)]
        s = [_dot_nt(kblk[hh // REP],
                     q_ref[:, hh * DSA_HEAD_DIM:(hh + 1) * DSA_HEAD_DIM]).astype(BF16) + bias
             for hh in heads]
        m_old = [m_ref[hh] for hh in heads]
        m_new = [jnp.maximum(m_old[hh], jnp.max(s[hh], axis=0, keepdims=True).astype(F32))
                 for hh in heads]
        alpha = [jnp.exp2(m_old[hh] - m_new[hh]) for hh in heads]
        p = [jnp.exp2(s[hh] - m_new[hh][0:1, :].astype(BF16)) for hh in heads]
        for hh in heads:
            part = tree_sum(p[hh], PACKED_ROWS).astype(F32)
            l_ref[hh] = alpha[hh] * l_ref[hh] + jnp.sum(part, axis=0, keepdims=True)
            m_ref[hh] = m_new[hh]
        pv = [_dot_tn(vblk[hh // REP], p[hh]) for hh in heads]
        for hh in heads:
            acc_ref[hh] = alpha[hh][0:1, :] * acc_ref[hh] + pv[hh]

    for_key_blocks(attend)

    for hh in range(DSA_HEADS):
        o_t = acc_ref[hh] / l_ref[hh][0:1, :]
        z = z_ref[:, hh * DSA_HEAD_DIM:(hh + 1) * DSA_HEAD_DIM].astype(F32)
        o_ref[:, hh * DSA_HEAD_DIM:(hh + 1) * DSA_HEAD_DIM] = (
            o_t.T * (z * _sigmoid(z))).astype(BF16)


def _dsa(main, small, k_idx, batch, seq):
    nq = seq // DSA_TQ
    k_sel = min(TOPK_MAX, seq // 4)
    width = DSA_HEADS * DSA_HEAD_DIM
    kvw = DSA_KV_HEADS * DSA_HEAD_DIM

    def tokw(col0):
        return pl.BlockSpec((DSA_TQ, width), lambda b, i, c=col0 // width: (b * nq + i, c))

    return pl.pallas_call(
        functools.partial(_dsa_kernel, k_sel=k_sel),
        grid=(batch, nq),
        in_specs=[
            tokw(OFF_AQ), tokw(OFF_IQ),
            pl.BlockSpec((DSA_TQ, LANES), lambda b, i: (b * nq + i, 0)),
            pl.BlockSpec((seq, kvw), lambda b, i: (b, OFF_AK // kvw)),
            pl.BlockSpec((seq, kvw), lambda b, i: (b, OFF_AV // kvw)),
            pl.BlockSpec((seq, IDX_DIM), lambda b, i: (b, 0)),
            tokw(OFF_AZ),
        ],
        out_specs=pl.BlockSpec((DSA_TQ, width), lambda b, i: (b * nq + i, 0)),
        out_shape=jax.ShapeDtypeStruct((batch * seq, width), BF16),
        scratch_shapes=[
            pltpu.VMEM((seq, DSA_TQ), F32),
            pltpu.VMEM((seq, DSA_TQ), BF16),
            pltpu.VMEM((IDX_HEADS, DSA_TQ, IDX_DIM), BF16),
            pltpu.VMEM((DSA_HEADS, DSA_HEAD_DIM, DSA_TQ), F32),
            pltpu.VMEM((DSA_HEADS, SUBLANES, DSA_TQ), F32),
            pltpu.VMEM((DSA_HEADS, SUBLANES, DSA_TQ), F32),
        ],
        compiler_params=pltpu.CompilerParams(
            dimension_semantics=("arbitrary", "arbitrary"), vmem_limit_bytes=VMEM_LIMIT),
        name="dsa",
    )(main, main, small, main, main, k_idx, main)


OUT_TM = 512


def _out_kernel(oa_ref, ob_ref, x_ref, w_ref, g_ref, out_ref):
    wa = oa_ref.shape[1]
    mixed = _dot(oa_ref[...], w_ref[0:wa, :]) + _dot(ob_ref[...], w_ref[wa:, :])
    y = x_ref[...] + mixed
    ms = jnp.mean(y * y, axis=-1, keepdims=True)
    out_ref[...] = y * lax.rsqrt(ms + EPS) * g_ref[...]


def _output_projection(o_a, o_b, xf, w_out, g):
    n = xf.shape[0]
    wa, wb = o_a.shape[1], o_b.shape[1]
    return pl.pallas_call(
        _out_kernel,
        grid=(n // OUT_TM,),
        in_specs=[
            pl.BlockSpec((OUT_TM, wa), lambda i: (i, 0)),
            pl.BlockSpec((OUT_TM, wb), lambda i: (i, 0)),
            pl.BlockSpec((OUT_TM, D_MODEL), lambda i: (i, 0)),
            pl.BlockSpec((wa + wb, D_MODEL), lambda i: (0, 0)),
            pl.BlockSpec((1, D_MODEL), lambda i: (0, 0)),
        ],
        out_specs=pl.BlockSpec((OUT_TM, D_MODEL), lambda i: (i, 0)),
        out_shape=jax.ShapeDtypeStruct((n, D_MODEL), F32),
        compiler_params=pltpu.CompilerParams(
            dimension_semantics=("arbitrary",), vmem_limit_bytes=VMEM_LIMIT),
        name="out_proj",
    )(o_a, o_b, xf, w_out, g)


def _rope_table(positions):
    lane = jnp.arange(LANES)
    rot128, rot64 = 2 * HALF128, 2 * HALF64
    j128 = (lane % HALF128).astype(F32) * 2.0 / rot128
    j64 = (lane % HALF64).astype(F32) * 2.0 / rot64
    freq = jnp.where(lane < rot128, ROPE_THETA ** (-j128),
                     jnp.where(lane < rot128 + rot64, ROPE_THETA ** (-j64), 0.0))
    first = jnp.where(lane < rot128, lane < HALF128, (lane - rot128) < HALF64)
    sign = jnp.where(first, -1.0, 1.0).astype(F32)
    ang = positions.astype(F32).reshape(-1, 1) * freq
    return jnp.cos(ang), jnp.sin(ang) * sign


def _split_w_in(w):
    w = w.astype(BF16)
    w_b = w[:, WB_SRC0:WB_SRC0 + WB_COLS]
    gab = w[:, WA_COLS:WB_SRC0]
    ikw = w[:, WB_SRC0 + WB_COLS:]
    pad = jnp.zeros((w.shape[0], LANES - gab.shape[1] - ikw.shape[1]), w.dtype)
    w_small = jnp.concatenate([ikw, gab, pad], axis=1)
    return w, w_b, w_small


def kernel(x, positions, attn_norm_g, w_in, gdn_conv_w, gdn_a_log, gdn_dt_bias, gdn_norm_g,
           w_out, final_norm_g):
    batch, seq, d = x.shape
    assert d == D_MODEL and w_in.shape[0] == 1, "single-layer trunk with D_MODEL=2048 only"
    assert seq % max(GDN_TC, DSA_TQ) == 0 and (batch * seq) % max(PROJ_TM, OUT_TM) == 0
    xf = x.reshape(batch * seq, d)

    w_a, w_b, w_small = _split_w_in(w_in[0])
    main, small, k_idx = _input_projection(xf, attn_norm_g[0].reshape(1, d), w_a, w_b, w_small,
                                           *_rope_table(positions))

    def head_rows(p):
        return jnp.broadcast_to(jnp.tile(p, GDN_NCH)[:, None], (GDN_R, LANES))

    o_a = _gdn(main, small, gdn_conv_w[0], head_rows(gdn_a_log[0]), head_rows(gdn_dt_bias[0]),
               gdn_norm_g[0].reshape(1, GDN_DV), batch, seq)
    o_b = _dsa(main, small, k_idx, batch, seq)

    out = _output_projection(o_a, o_b, xf, w_out[0].astype(BF16), final_norm_g.reshape(1, d))
    return out.reshape(batch, seq, d)
```
